```python
import math
import jax, jax.numpy as jnp
from jax import lax
import numpy as np

D_MODEL = 2048
BATCH = 4
SEQ = 2048
DEPTH = 4
DEC_BATCH = 128
DEC_SEQ = 1
PAST_LEN = 16384
PAGE_SIZE = 128

N_QK_HEADS = 16
N_V_HEADS = 32
HEAD_K = 128
HEAD_V = 128
KEY_DIM = N_QK_HEADS * HEAD_K
VAL_DIM = N_V_HEADS * HEAD_V
QKV_DIM = 2 * KEY_DIM + VAL_DIM
QKV_CONV = 4
CHUNK = 64
SC_DIM = D_MODEL
SC_CONV = 3
D_FF = ((8 * D_MODEL // 3 + 255) // 256) * 256
PLE_DIM = 256
EPS = 1e-6

IN_SIZES = (KEY_DIM, KEY_DIM, VAL_DIM, VAL_DIM, N_V_HEADS, N_V_HEADS,
            SC_DIM, SC_DIM, SC_DIM, D_MODEL, D_MODEL)
IN_DIM = sum(IN_SIZES)
IN_SPLITS = tuple(int(s) for s in np.cumsum(IN_SIZES)[:-1])

kernel_name = "hybrid_gdn_shortconv_decode_step"


def rmsnorm(x, g):
    xf = x.astype(jnp.float32)
    r = xf * lax.rsqrt(jnp.mean(xf * xf, axis=-1, keepdims=True) + EPS)
    return (r * g.astype(jnp.float32)).astype(x.dtype)


def l2norm(x):
    return x * lax.rsqrt(jnp.sum(x * x, axis=-1, keepdims=True) + EPS)


def causal_dwconv(x, buf, w):
    K = w.shape[0]
    T = x.shape[1]
    xp = jnp.concatenate([buf.astype(x.dtype), x], axis=1)
    y = sum(w[j].astype(x.dtype) * xp[:, j:j + T] for j in range(K))
    return y, xp[:, -(K - 1):]


def gated_delta_rule(q, k, v, g, beta, s0):
    Bn, T, H, Dk = q.shape
    Dv = v.shape[-1]
    n_chunks = -(-T // CHUNK)
    pad = n_chunks * CHUNK - T

    def prep(a):
        a = jnp.pad(a, [(0, 0), (0, pad)] + [(0, 0)] * (a.ndim - 2))
        a = a.reshape((Bn, n_chunks, CHUNK) + a.shape[2:])
        return jnp.moveaxis(a, 3, 1)

    q = prep(q) * (HEAD_K ** -0.5)
    k = prep(k)
    v = prep(v)
    g = prep(g)
    beta = prep(beta)
    G = jnp.cumsum(g, axis=-1)
    idx = jnp.arange(CHUNK)
    lower = idx[:, None] >= idx[None, :]
    strict = idx[:, None] > idx[None, :]
    decay = jnp.exp(jnp.where(lower, G[..., :, None] - G[..., None, :], -jnp.inf))
    kk = jnp.einsum('bhncd,bhnsd->bhncs', k, k)
    A = jnp.where(strict, beta[..., :, None] * kk * decay, 0.0)
    eye = jnp.eye(CHUNK, dtype=A.dtype)
    rhs = jnp.concatenate([v * beta[..., None], k * (beta * jnp.exp(G))[..., None]], axis=-1)
    sol = lax.linalg.triangular_solve(eye + A, rhs, left_side=True, lower=True)
    u = sol[..., :Dv]
    w = sol[..., Dv:]
    qk = jnp.einsum('bhncd,bhnsd->bhncs', q, k) * decay

    def step(S, xs):
        q_c, k_c, u_c, w_c, G_c, qk_c = xs
        u_new = u_c - jnp.einsum('bhcd,bhde->bhce', w_c, S)
        o = (jnp.einsum('bhcd,bhde->bhce', q_c * jnp.exp(G_c)[..., None], S)
             + jnp.einsum('bhcs,bhse->bhce', qk_c, u_new))
        G_last = G_c[..., -1]
        S = (S * jnp.exp(G_last)[..., None, None]
             + jnp.einsum('bhcd,bhce->bhde', k_c * jnp.exp(G_last[..., None] - G_c)[..., None], u_new))
        return S, o

    xs = tuple(jnp.moveaxis(a, 2, 0) for a in (q, k, u, w, G, qk))
    S, o = lax.scan(step, s0, xs)
    o = jnp.moveaxis(jnp.moveaxis(o, 0, 2), 1, 3).reshape(Bn, n_chunks * CHUNK, H, Dv)[:, :T]
    return o, S


def layer(h, p_i, s_gdn, s_qkv, s_sc,
          attn_norm, w_in, qkv_conv_w, a_log, dt_bias, gdn_norm, w_gdn_out,
          sconv_w, w_sconv_out, w_o, ffn_norm, w_ffn_gate, w_ffn_up, w_ffn_down,
          ple_norm, w_ple_gate, w_ple_proj):
    Bn, T, _ = h.shape
    dt = h.dtype
    xn = rmsnorm(h, attn_norm)
    proj = xn @ w_in
    q, k, v, z, a, b, sB, sC, sh, gA, gB = jnp.split(proj, IN_SPLITS, axis=-1)

    qkv, new_qkv = causal_dwconv(jnp.concatenate([q, k, v], axis=-1), s_qkv, qkv_conv_w)
    qkv = jax.nn.silu(qkv).astype(jnp.float32)
    q, k, v = jnp.split(qkv, (KEY_DIM, 2 * KEY_DIM), axis=-1)
    rep = N_V_HEADS // N_QK_HEADS
    q = jnp.repeat(l2norm(q.reshape(Bn, T, N_QK_HEADS, HEAD_K)), rep, axis=2)
    k = jnp.repeat(l2norm(k.reshape(Bn, T, N_QK_HEADS, HEAD_K)), rep, axis=2)
    v = v.reshape(Bn, T, N_V_HEADS, HEAD_V)
    beta = jax.nn.sigmoid(b.astype(jnp.float32))
    g = -jnp.exp(a_log.astype(jnp.float32)) * jax.nn.softplus(a.astype(jnp.float32) + dt_bias.astype(jnp.float32))
    o, S = gated_delta_rule(q, k, v, g, beta, s_gdn.astype(jnp.float32))
    o = rmsnorm(o, gdn_norm) * jax.nn.silu(z.reshape(Bn, T, N_V_HEADS, HEAD_V).astype(jnp.float32))
    y_a = o.reshape(Bn, T, VAL_DIM).astype(dt) @ w_gdn_out

    conv, new_sc = causal_dwconv(sC * sh, s_sc, sconv_w)
    y_b = (sB * conv) @ w_sconv_out

    mixed = jax.nn.sigmoid(gA) * y_a + jax.nn.sigmoid(gB) * y_b
    h = h + mixed @ w_o

    hn = rmsnorm(h, ffn_norm)
    h = h + (jax.nn.silu(hn @ w_ffn_gate) * (hn @ w_ffn_up)) @ w_ffn_down

    hn = rmsnorm(h, ple_norm)
    h = h + jax.nn.sigmoid(hn @ w_ple_gate) * (p_i.astype(dt) @ w_ple_proj)
    return h, S.astype(s_gdn.dtype), new_qkv.astype(s_qkv.dtype), new_sc.astype(s_sc.dtype)


def setup_inputs(seed: int = 0) -> dict:
    key = jax.random.key(seed)
    ks = jax.random.split(key, 32)
    f32 = jnp.float32

    def nrm(k, shape, scale):
        return jax.random.normal(k, shape, f32) * scale

    def gain(k, shape):
        return 1.0 + 0.02 * jax.random.normal(k, shape, f32)

    return {
        "x_prompt": nrm(ks[0], (BATCH, SEQ, D_MODEL), 1.0),
        "x_sample": nrm(ks[1], (DEC_BATCH, DEC_SEQ, D_MODEL), 1.0),
        "state_gdn": nrm(ks[2], (DEPTH, DEC_BATCH, N_V_HEADS, HEAD_K, HEAD_V), 0.05),
        "state_qkv_conv": nrm(ks[3], (DEPTH, DEC_BATCH, QKV_CONV - 1, QKV_DIM), 1.0),
        "state_short_conv": nrm(ks[4], (DEPTH, DEC_BATCH, SC_CONV - 1, SC_DIM), 1.0),
        "p_prompt": nrm(ks[5], (DEPTH, BATCH, SEQ, PLE_DIM), 1.0),
        "p_sample": nrm(ks[6], (DEPTH, DEC_BATCH, DEC_SEQ, PLE_DIM), 1.0),
        "attn_norm": gain(ks[7], (DEPTH, D_MODEL)),
        "w_in": nrm(ks[8], (DEPTH, D_MODEL, IN_DIM), D_MODEL ** -0.5),
        "qkv_conv_w": nrm(ks[9], (DEPTH, QKV_CONV, QKV_DIM), QKV_CONV ** -0.5),
        "a_log": jnp.log(jax.random.uniform(ks[10], (DEPTH, N_V_HEADS), f32, 1.0, 16.0)),
        "dt_bias": nrm(ks[11], (DEPTH, N_V_HEADS), 0.1),
        "gdn_norm": gain(ks[12], (DEPTH, HEAD_V)),
        "w_gdn_out": nrm(ks[13], (DEPTH, VAL_DIM, D_MODEL), VAL_DIM ** -0.5),
        "sconv_w": nrm(ks[14], (DEPTH, SC_CONV, SC_DIM), SC_CONV ** -0.5),
        "w_sconv_out": nrm(ks[15], (DEPTH, SC_DIM, D_MODEL), SC_DIM ** -0.5),
        "w_o": nrm(ks[16], (DEPTH, D_MODEL, D_MODEL), D_MODEL ** -0.5),
        "ffn_norm": gain(ks[17], (DEPTH, D_MODEL)),
        "w_ffn_gate": nrm(ks[18], (DEPTH, D_MODEL, D_FF), D_MODEL ** -0.5),
        "w_ffn_up": nrm(ks[19], (DEPTH, D_MODEL, D_FF), D_MODEL ** -0.5),
        "w_ffn_down": nrm(ks[20], (DEPTH, D_FF, D_MODEL), D_FF ** -0.5),
        "ple_norm": gain(ks[21], (DEPTH, D_MODEL)),
        "w_ple_gate": nrm(ks[22], (DEPTH, D_MODEL, D_MODEL), D_MODEL ** -0.5),
        "w_ple_proj": nrm(ks[23], (DEPTH, PLE_DIM, D_MODEL), PLE_DIM ** -0.5),
        "final_norm": gain(ks[24], (D_MODEL,)),
    }


def reference(x_prompt, x_sample, state_gdn, state_qkv_conv, state_short_conv, p_prompt, p_sample,
              attn_norm, w_in, qkv_conv_w, a_log, dt_bias, gdn_norm, w_gdn_out, sconv_w, w_sconv_out,
              w_o, ffn_norm, w_ffn_gate, w_ffn_up, w_ffn_down, ple_norm, w_ple_gate, w_ple_proj,
              final_norm):
    nb = x_prompt.shape[0]
    dt = x_prompt.dtype
    zero_gdn = jnp.zeros((nb, N_V_HEADS, HEAD_K, HEAD_V), dt)
    zero_qkv = jnp.zeros((nb, QKV_CONV - 1, QKV_DIM), dt)
    zero_sc = jnp.zeros((nb, SC_CONV - 1, SC_DIM), dt)
    hp, hs = x_prompt, x_sample
    gdn_p, qkv_p, sc_p, gdn_s, qkv_s, sc_s = [], [], [], [], [], []
    for i in range(DEPTH):
        lw = (attn_norm[i], w_in[i], qkv_conv_w[i], a_log[i], dt_bias[i], gdn_norm[i], w_gdn_out[i],
              sconv_w[i], w_sconv_out[i], w_o[i], ffn_norm[i], w_ffn_gate[i], w_ffn_up[i], w_ffn_down[i],
              ple_norm[i], w_ple_gate[i], w_ple_proj[i])
        hp, s1, s2, s3 = layer(hp, p_prompt[i], zero_gdn, zero_qkv, zero_sc, *lw)
        gdn_p.append(s1); qkv_p.append(s2); sc_p.append(s3)
        hs, s1, s2, s3 = layer(hs, p_sample[i], state_gdn[i], state_qkv_conv[i], state_short_conv[i], *lw)
        gdn_s.append(s1); qkv_s.append(s2); sc_s.append(s3)
    y_prompt = rmsnorm(hp, final_norm)
    y_sample = rmsnorm(hs, final_norm)
    return (y_prompt, y_sample, jnp.stack(gdn_p), jnp.stack(qkv_p), jnp.stack(sc_p),
            jnp.stack(gdn_s), jnp.stack(qkv_s), jnp.stack(sc_s))
```

```python
import functools

import jax
import jax.numpy as jnp
from jax import lax
from jax.experimental import pallas as pl
from jax.experimental.pallas import tpu as pltpu

F32 = jnp.float32
BF16 = jnp.bfloat16
HI = lax.Precision.HIGHEST

D_MODEL = 2048
DEPTH = 4
N_QK_HEADS = 16
N_V_HEADS = 32
HEAD_K = 128
HEAD_V = 128
KEY_DIM = N_QK_HEADS * HEAD_K
VAL_DIM = N_V_HEADS * HEAD_V
QKV_DIM = 2 * KEY_DIM + VAL_DIM
QKV_CONV = 4
SC_CONV = 3
SC_DIM = D_MODEL
CHUNK = 64
D_FF = 5632
PLE_DIM = 256
EPS = 1e-6

OFF_Z = QKV_DIM
OFF_AB = OFF_Z + VAL_DIM
OFF_REST = OFF_AB + 2 * N_V_HEADS
REST_SB, REST_SC, REST_SH, REST_G = 0, SC_DIM, 2 * SC_DIM, 3 * SC_DIM

VMEM_LIMIT_BYTES = 56 * 1024 * 1024
SUBLANES = 8

GDN_HEADS_PER_STEP = 4
SAMPLE_TILE = 8


def _params(*sem):
    return pltpu.CompilerParams(dimension_semantics=sem, vmem_limit_bytes=VMEM_LIMIT_BYTES)


def _dot(a, b):
    return jnp.dot(a, b, preferred_element_type=F32)


def _dot_nt(a, b):
    return lax.dot_general(a, b, (((1,), (1,)), ((), ())), preferred_element_type=F32)


def _dot_tn(a, b, precision=None):
    return lax.dot_general(a, b, (((0,), (0,)), ((), ())), precision=precision,
                           preferred_element_type=F32)


def _dot_hi(a, b):
    return jnp.dot(a, b, precision=HI, preferred_element_type=F32)


def _rms(x, g):
    return x * lax.rsqrt(jnp.mean(x * x, axis=-1, keepdims=True) + EPS) * g


def _silu(x):
    return x * jax.nn.sigmoid(x)


def _rmsnorm_kernel(x_ref, g_ref, o_ref):
    o_ref[...] = _rms(x_ref[...], g_ref[...]).astype(o_ref.dtype)


def _rmsnorm_call(x, gain, li, tm):
    m, d = x.shape
    return pl.pallas_call(
        _rmsnorm_kernel,
        grid=(m // tm,),
        in_specs=[pl.BlockSpec((tm, d), lambda i: (i, 0)),
                  pl.BlockSpec((None, 1, d), lambda i: (li, 0, 0))],
        out_specs=pl.BlockSpec((tm, d), lambda i: (i, 0)),
        out_shape=jax.ShapeDtypeStruct((m, d), BF16),
        compiler_params=_params("parallel"),
        name="rmsnorm",
    )(x, gain)


def _proj_act_kernel(x_ref, w_ref, o_ref, *, act):
    o_ref[...] = act(_dot(x_ref[...], w_ref[...])).astype(o_ref.dtype)


def _proj_act_call(x, w, li, col0, n_out, act, tm, tn, name):
    m, k = x.shape
    cb = col0 // tn
    return pl.pallas_call(
        functools.partial(_proj_act_kernel, act=act),
        grid=(n_out // tn, m // tm),
        in_specs=[pl.BlockSpec((tm, k), lambda n, i: (i, 0)),
                  pl.BlockSpec((None, k, tn), lambda n, i: (li, 0, cb + n))],
        out_specs=pl.BlockSpec((tm, tn), lambda n, i: (i, n)),
        out_shape=jax.ShapeDtypeStruct((m, n_out), F32),
        compiler_params=_params("parallel", "parallel"),
        name=name,
    )(x, w)


def _gates_kernel(x_ref, w_ref, alog_ref, dtb_ref, o_ref):
    r = _dot(x_ref[...], w_ref[...])
    a = r[:, :N_V_HEADS] + dtb_ref[...]
    softplus = jnp.maximum(a, 0.0) + jnp.log(1.0 + jnp.exp(-jnp.abs(a)))
    g = -jnp.exp(alog_ref[...]) * softplus
    o_ref[:, 0:N_V_HEADS] = g
    o_ref[:, N_V_HEADS:2 * N_V_HEADS] = jax.nn.sigmoid(r[:, N_V_HEADS:])
    o_ref[:, 2 * N_V_HEADS:3 * N_V_HEADS] = jnp.exp(g)


def _gates_call(x, w_ab, a_log, dt_bias, li, tm):
    m, k = x.shape
    h2 = 2 * N_V_HEADS
    return pl.pallas_call(
        _gates_kernel,
        grid=(m // tm,),
        in_specs=[pl.BlockSpec((tm, k), lambda i: (i, 0)),
                  pl.BlockSpec((None, k, h2), lambda i: (li, 0, 0)),
                  pl.BlockSpec((None, 1, N_V_HEADS), lambda i: (li, 0, 0)),
                  pl.BlockSpec((None, 1, N_V_HEADS), lambda i: (li, 0, 0))],
        out_specs=pl.BlockSpec((tm, 3 * N_V_HEADS), lambda i: (i, 0)),
        out_shape=jax.ShapeDtypeStruct((m, 3 * N_V_HEADS), F32),
        compiler_params=_params("parallel"),
        name="gates",
    )(x, w_ab, a_log, dt_bias)


def _qkv_prompt_kernel(x_ref, w_ref, cw_ref, act_ref, st_ref, xs_ref, *, tm, tiles_per_seq):
    i = pl.program_id(1)
    x = _dot(x_ref[...], w_ref[...])

    @pl.when(i % tiles_per_seq == 0)
    def _():
        xs_ref[0:SUBLANES, :] = jnp.zeros((SUBLANES, xs_ref.shape[1]), F32)

    xs_ref[SUBLANES:tm + SUBLANES, :] = x
    cw = cw_ref[...]
    y = cw[3:4] * x
    for j in range(QKV_CONV - 1):
        lag = QKV_CONV - 1 - j
        y = y + cw[j:j + 1] * xs_ref[SUBLANES - lag:tm + SUBLANES - lag, :]
    act_ref[...] = _silu(y)
    st_ref[...] = xs_ref[tm + SUBLANES - (QKV_CONV - 1):tm + SUBLANES, :]
    xs_ref[0:SUBLANES, :] = xs_ref[tm:tm + SUBLANES, :]


def _qkv_prompt_call(x, w, conv_w, li, seq, tm, tn):
    m, k = x.shape
    tps = seq // tm
    return pl.pallas_call(
        functools.partial(_qkv_prompt_kernel, tm=tm, tiles_per_seq=tps),
        grid=(QKV_DIM // tn, m // tm),
        in_specs=[pl.BlockSpec((tm, k), lambda n, i: (i, 0)),
                  pl.BlockSpec((None, k, tn), lambda n, i: (li, 0, n)),
                  pl.BlockSpec((None, QKV_CONV, tn), lambda n, i: (li, 0, n))],
        out_specs=[pl.BlockSpec((tm, tn), lambda n, i: (i, n)),
                   pl.BlockSpec((None, QKV_CONV - 1, tn), lambda n, i: (i // tps, 0, n))],
        out_shape=[jax.ShapeDtypeStruct((m, QKV_DIM), F32),
                   jax.ShapeDtypeStruct((m // seq, QKV_CONV - 1, QKV_DIM), F32)],
        scratch_shapes=[pltpu.VMEM((tm + SUBLANES, tn), F32)],
        compiler_params=_params("arbitrary", "arbitrary"),
        name="qkv_conv_prompt",
    )(x, w, conv_w)


def _qkv_sample_kernel(x_ref, w_ref, cw_ref, s_ref, act_ref, st_ref):
    x = _dot(x_ref[...], w_ref[...])
    cw = cw_ref[...]
    y = cw[3:4] * x
    for j in range(QKV_CONV - 1):
        y = y + cw[j:j + 1] * s_ref[j]
    act_ref[...] = _silu(y)
    st_ref[0] = s_ref[1]
    st_ref[1] = s_ref[2]
    st_ref[2] = x


def _qkv_sample_call(x, w, conv_w, state_t, li, tn):
    m, k = x.shape
    return pl.pallas_call(
        _qkv_sample_kernel,
        grid=(QKV_DIM // tn,),
        in_specs=[pl.BlockSpec((m, k), lambda n: (0, 0)),
                  pl.BlockSpec((None, k, tn), lambda n: (li, 0, n)),
                  pl.BlockSpec((None, QKV_CONV, tn), lambda n: (li, 0, n)),
                  pl.BlockSpec((None, QKV_CONV - 1, m, tn), lambda n: (li, 0, 0, n))],
        out_specs=[pl.BlockSpec((m, tn), lambda n: (0, n)),
                   pl.BlockSpec((QKV_CONV - 1, m, tn), lambda n: (0, 0, n))],
        out_shape=[jax.ShapeDtypeStruct((m, QKV_DIM), F32),
                   jax.ShapeDtypeStruct((QKV_CONV - 1, m, QKV_DIM), F32)],
        compiler_params=_params("parallel"),
        name="qkv_conv_sample",
    )(x, w, conv_w, state_t)


def _sconv_prompt_kernel(x_ref, wb_ref, wc_ref, wh_ref, cw_ref, y_ref, st_ref, us_ref, *,
                         tm, tiles_per_seq):
    i = pl.program_id(1)
    x = x_ref[...]
    u = _dot(x, wc_ref[...]) * _dot(x, wh_ref[...])

    @pl.when(i % tiles_per_seq == 0)
    def _():
        us_ref[0:SUBLANES, :] = jnp.zeros((SUBLANES, us_ref.shape[1]), F32)

    us_ref[SUBLANES:tm + SUBLANES, :] = u
    cw = cw_ref[...]
    conv = cw[2:3] * u
    for j in range(SC_CONV - 1):
        lag = SC_CONV - 1 - j
        conv = conv + cw[j:j + 1] * us_ref[SUBLANES - lag:tm + SUBLANES - lag, :]
    y_ref[...] = (_dot(x, wb_ref[...]) * conv).astype(y_ref.dtype)
    st_ref[...] = us_ref[tm + SUBLANES - (SC_CONV - 1):tm + SUBLANES, :]
    us_ref[0:SUBLANES, :] = us_ref[tm:tm + SUBLANES, :]


def _sconv_prompt_call(x, w_rest, conv_w, li, seq, tm, tn):
    m, k = x.shape
    tps = seq // tm
    cb, cc, ch = REST_SB // tn, REST_SC // tn, REST_SH // tn
    return pl.pallas_call(
        functools.partial(_sconv_prompt_kernel, tm=tm, tiles_per_seq=tps),
        grid=(SC_DIM // tn, m // tm),
        in_specs=[pl.BlockSpec((tm, k), lambda n, i: (i, 0)),
                  pl.BlockSpec((None, k, tn), lambda n, i: (li, 0, cb + n)),
                  pl.BlockSpec((None, k, tn), lambda n, i: (li, 0, cc + n)),
                  pl.BlockSpec((None, k, tn), lambda n, i: (li, 0, ch + n)),
                  pl.BlockSpec((None, SC_CONV, tn), lambda n, i: (li, 0, n))],
        out_specs=[pl.BlockSpec((tm, tn), lambda n, i: (i, n)),
                   pl.BlockSpec((None, SC_CONV - 1, tn), lambda n, i: (i // tps, 0, n))],
        out_shape=[jax.ShapeDtypeStruct((m, SC_DIM), BF16),
                   jax.ShapeDtypeStruct((m // seq, SC_CONV - 1, SC_DIM), F32)],
        scratch_shapes=[pltpu.VMEM((tm + SUBLANES, tn), F32)],
        compiler_params=_params("arbitrary", "arbitrary"),
        name="sconv_prompt",
    )(x, w_rest, w_rest, w_rest, conv_w)


def _sconv_sample_kernel(x_ref, wb_ref, wc_ref, wh_ref, cw_ref, s_ref, y_ref, st_ref):
    x = x_ref[...]
    u = _dot(x, wc_ref[...]) * _dot(x, wh_ref[...])
    cw = cw_ref[...]
    conv = cw[2:3] * u
    for j in range(SC_CONV - 1):
        conv = conv + cw[j:j + 1] * s_ref[j]
    y_ref[...] = (_dot(x, wb_ref[...]) * conv).astype(y_ref.dtype)
    st_ref[0] = s_ref[1]
    st_ref[1] = u


def _sconv_sample_call(x, w_rest, conv_w, state_t, li, tn):
    m, k = x.shape
    cb, cc, ch = REST_SB // tn, REST_SC // tn, REST_SH // tn
    return pl.pallas_call(
        _sconv_sample_kernel,
        grid=(SC_DIM // tn,),
        in_specs=[pl.BlockSpec((m, k), lambda n: (0, 0)),
                  pl.BlockSpec((None, k, tn), lambda n: (li, 0, cb + n)),
                  pl.BlockSpec((None, k, tn), lambda n: (li, 0, cc + n)),
                  pl.BlockSpec((None, k, tn), lambda n: (li, 0, ch + n)),
                  pl.BlockSpec((None, SC_CONV, tn), lambda n: (li, 0, n)),
                  pl.BlockSpec((None, SC_CONV - 1, m, tn), lambda n: (li, 0, 0, n))],
        out_specs=[pl.BlockSpec((m, tn), lambda n: (0, n)),
                   pl.BlockSpec((SC_CONV - 1, m, tn), lambda n: (0, 0, n))],
        out_shape=[jax.ShapeDtypeStruct((m, SC_DIM), BF16),
                   jax.ShapeDtypeStruct((SC_CONV - 1, m, SC_DIM), F32)],
        compiler_params=_params("parallel"),
        name="sconv_sample",
    )(x, w_rest, w_rest, w_rest, conv_w, state_t)


def _gdn_prompt_kernel(q_ref, k_ref, v_ref, z_ref, gb_ref, gn_ref, y_ref, s_ref, *, hb):
    c = CHUNK

    @pl.when(pl.program_id(2) == 0)
    def _():
        s_ref[...] = jnp.zeros(s_ref.shape, F32)

    ri = lax.broadcasted_iota(jnp.int32, (c, c), 0)
    ci = lax.broadcasted_iota(jnp.int32, (c, c), 1)
    lower = ri >= ci
    strict = ri > ci
    eye = (ri == ci).astype(F32)

    gb = gb_ref[...]
    g = gb[:, :hb]
    beta = gb[:, hb:]
    g_col = _dot_hi(lower.astype(F32), g)
    g_row = _dot_tn(g, (ri <= ci).astype(F32), precision=HI)
    g_last = g_col[c - 1:c, :]
    e_g = jnp.exp(g_col)
    e_rest = jnp.exp(g_last - g_col)
    e_last = jnp.exp(g_last)
    beta_e_g = beta * e_g
    gn = gn_ref[...]

    for j in range(hb // 2):
        q = q_ref[:, j * HEAD_K:(j + 1) * HEAD_K]
        k = k_ref[:, j * HEAD_K:(j + 1) * HEAD_K]
        qn = q * (lax.rsqrt(jnp.sum(q * q, axis=-1, keepdims=True) + EPS) * (HEAD_K ** -0.5))
        kn = k * lax.rsqrt(jnp.sum(k * k, axis=-1, keepdims=True) + EPS)
        kn_b = kn.astype(BF16)
        kk = _dot_nt(kn_b, kn_b)
        qk = _dot_nt(qn.astype(BF16), kn_b)
        for r in range(2):
            h = 2 * j + r
            decay = jnp.exp(jnp.where(lower, g_col[:, h:h + 1] - g_row[h:h + 1, :], -jnp.inf))
            a = jnp.where(strict, beta[:, h:h + 1] * kk * decay, 0.0)
            inv = eye - a
            p = _dot_hi(a, a)
            span = 2
            while span < c:
                inv = inv + _dot_hi(inv, p)
                span *= 2
                if span < c:
                    p = _dot_hi(p, p)
            v = v_ref[:, h * HEAD_V:(h + 1) * HEAD_V]
            rhs = jnp.concatenate([v * beta[:, h:h + 1], kn * beta_e_g[:, h:h + 1]], axis=1)
            sol = _dot_hi(inv, rhs)
            u = sol[:, :HEAD_V]
            w = sol[:, HEAD_V:]
            s = s_ref[h]
            s_b = s.astype(BF16)
            u_new = u - _dot(w.astype(BF16), s_b)
            u_b = u_new.astype(BF16)
            o = _dot((qn * e_g[:, h:h + 1]).astype(BF16), s_b) + _dot((qk * decay).astype(BF16), u_b)
            s_ref[h] = s * e_last[:, h:h + 1] + _dot_tn((kn * e_rest[:, h:h + 1]).astype(BF16), u_b)
            y = _rms(o, gn) * z_ref[:, h * HEAD_V:(h + 1) * HEAD_V]
            y_ref[:, h * HEAD_V:(h + 1) * HEAD_V] = y.astype(y_ref.dtype)


def _gdn_prompt_call(qkv, z_act, gb_blocks, gdn_norm, li, seq, hb):
    m = qkv.shape[0]
    nb, nc = m // seq, seq // CHUNK
    qw, vw = HEAD_K * hb // 2, HEAD_V * hb
    kb, vb = KEY_DIM // qw, 2 * KEY_DIM // vw
    return pl.pallas_call(
        functools.partial(_gdn_prompt_kernel, hb=hb),
        grid=(nb, N_V_HEADS // hb, nc),
        in_specs=[pl.BlockSpec((CHUNK, qw), lambda b, h, c: (b * nc + c, h)),
                  pl.BlockSpec((CHUNK, qw), lambda b, h, c: (b * nc + c, kb + h)),
                  pl.BlockSpec((CHUNK, vw), lambda b, h, c: (b * nc + c, vb + h)),
                  pl.BlockSpec((CHUNK, vw), lambda b, h, c: (b * nc + c, h)),
                  pl.BlockSpec((None, CHUNK, 2 * hb), lambda b, h, c: (h, b * nc + c, 0)),
                  pl.BlockSpec((None, 1, HEAD_V), lambda b, h, c: (li, 0, 0))],
        out_specs=[pl.BlockSpec((CHUNK, vw), lambda b, h, c: (b * nc + c, h)),
                   pl.BlockSpec((None, hb, HEAD_K, HEAD_V), lambda b, h, c: (b, h, 0, 0))],
        out_shape=[jax.ShapeDtypeStruct((m, VAL_DIM), BF16),
                   jax.ShapeDtypeStruct((nb, N_V_HEADS, HEAD_K, HEAD_V), F32)],
        compiler_params=_params("parallel", "parallel", "arbitrary"),
        name="gdn_prompt",
    )(qkv, qkv, qkv, z_act, gb_blocks, gdn_norm)


def _lanes_to_sublanes(row):
    n = row.shape[1]
    ri = lax.broadcasted_iota(jnp.int32, (n, n), 0)
    ci = lax.broadcasted_iota(jnp.int32, (n, n), 1)
    diag = jnp.where(ri == ci, jnp.broadcast_to(row, (n, n)), 0.0)
    return jnp.broadcast_to(jnp.sum(diag, axis=1, keepdims=True), (n, n))


def _gdn_sample_kernel(gate_ref, q_ref, k_ref, v_ref, z_ref, gn_ref, s_ref, y_ref, so_ref, *, hb, bt):
    hblk = pl.program_id(0)
    tile = pl.program_id(1)
    gn = gn_ref[...]
    for b in range(bt):
        base = (tile * bt + b) * (2 * N_V_HEADS) + hblk * hb
        for j in range(hb // 2):
            q = q_ref[b:b + 1, j * HEAD_K:(j + 1) * HEAD_K]
            k = k_ref[b:b + 1, j * HEAD_K:(j + 1) * HEAD_K]
            qn = q * (lax.rsqrt(jnp.sum(q * q, axis=-1, keepdims=True) + EPS) * (HEAD_K ** -0.5))
            kn = k * lax.rsqrt(jnp.sum(k * k, axis=-1, keepdims=True) + EPS)
            q_col = _lanes_to_sublanes(qn)
            k_col = _lanes_to_sublanes(kn)
            for r in range(2):
                h = 2 * j + r
                beta = gate_ref[base + h]
                e_g = gate_ref[base + N_V_HEADS + h]
                s = s_ref[b, h]
                s_k = jnp.sum(s * k_col, axis=0, keepdims=True)
                delta = beta * (v_ref[b:b + 1, h * HEAD_V:(h + 1) * HEAD_V] - e_g * s_k)
                s_new = e_g * s + k_col * delta
                so_ref[b, h] = s_new
                o = jnp.sum(s_new * q_col, axis=0, keepdims=True)
                y = _rms(o, gn) * z_ref[b:b + 1, h * HEAD_V:(h + 1) * HEAD_V]
                y_ref[b:b + 1, h * HEAD_V:(h + 1) * HEAD_V] = y


def _gdn_sample_call(gates_flat, qkv, z_act, gdn_norm, state, li, hb, bt):
    m = qkv.shape[0]
    qw, vw = HEAD_K * hb // 2, HEAD_V * hb
    kb, vb = KEY_DIM // qw, 2 * KEY_DIM // vw
    return pl.pallas_call(
        functools.partial(_gdn_sample_kernel, hb=hb, bt=bt),
        grid=(N_V_HEADS // hb, m // bt),
        in_specs=[pl.BlockSpec(memory_space=pltpu.SMEM),
                  pl.BlockSpec((bt, qw), lambda h, t: (t, h)),
                  pl.BlockSpec((bt, qw), lambda h, t: (t, kb + h)),
                  pl.BlockSpec((bt, vw), lambda h, t: (t, vb + h)),
                  pl.BlockSpec((bt, vw), lambda h, t: (t, h)),
                  pl.BlockSpec((None, 1, HEAD_V), lambda h, t: (li, 0, 0)),
                  pl.BlockSpec((None, bt, hb, HEAD_K, HEAD_V), lambda h, t: (li, t, h, 0, 0))],
        out_specs=[pl.BlockSpec((bt, vw), lambda h, t: (t, h)),
                   pl.BlockSpec((bt, hb, HEAD_K, HEAD_V), lambda h, t: (t, h, 0, 0))],
        out_shape=[jax.ShapeDtypeStruct((m, VAL_DIM), F32),
                   jax.ShapeDtypeStruct((m, N_V_HEADS, HEAD_K, HEAD_V), F32)],
        compiler_params=_params("parallel", "parallel"),
        name="gdn_sample",
    )(gates_flat, qkv, qkv, qkv, z_act, gdn_norm, state)


def _merge_kernel(ya_ref, yb_ref, ga_ref, gb_ref, wa_ref, wb_ref, o_ref):
    a = _dot(ya_ref[...].astype(BF16), wa_ref[...])
    b = _dot(yb_ref[...], wb_ref[...])
    o_ref[...] = (ga_ref[...] * a + gb_ref[...] * b).astype(o_ref.dtype)


def _merge_call(ya, yb, gates, w_gdn_out, w_sconv_out, li, tm, tn):
    m = ya.shape[0]
    gbo = D_MODEL // tn
    return pl.pallas_call(
        _merge_kernel,
        grid=(D_MODEL // tn, m // tm),
        in_specs=[pl.BlockSpec((tm, VAL_DIM), lambda n, i: (i, 0)),
                  pl.BlockSpec((tm, SC_DIM), lambda n, i: (i, 0)),
                  pl.BlockSpec((tm, tn), lambda n, i: (i, n)),
                  pl.BlockSpec((tm, tn), lambda n, i: (i, gbo + n)),
                  pl.BlockSpec((None, VAL_DIM, tn), lambda n, i: (li, 0, n)),
                  pl.BlockSpec((None, SC_DIM, tn), lambda n, i: (li, 0, n))],
        out_specs=pl.BlockSpec((tm, tn), lambda n, i: (i, n)),
        out_shape=jax.ShapeDtypeStruct((m, D_MODEL), BF16),
        compiler_params=_params("parallel", "parallel"),
        name="merge",
    )(ya, yb, gates, gates, w_gdn_out, w_sconv_out)


def _oproj_kernel(x_ref, h_ref, w_ref, g_ref, h_out, n_out):
    h = h_ref[...] + _dot(x_ref[...], w_ref[...])
    h_out[...] = h
    n_out[...] = _rms(h, g_ref[...]).astype(n_out.dtype)


def _oproj_call(x, h, w_o, ffn_norm, li, tm):
    m, d = h.shape
    return pl.pallas_call(
        _oproj_kernel,
        grid=(m // tm,),
        in_specs=[pl.BlockSpec((tm, d), lambda i: (i, 0)),
                  pl.BlockSpec((tm, d), lambda i: (i, 0)),
                  pl.BlockSpec((None, d, d), lambda i: (li, 0, 0)),
                  pl.BlockSpec((None, 1, d), lambda i: (li, 0, 0))],
        out_specs=[pl.BlockSpec((tm, d), lambda i: (i, 0)),
                   pl.BlockSpec((tm, d), lambda i: (i, 0))],
        out_shape=[jax.ShapeDtypeStruct((m, d), F32),
                   jax.ShapeDtypeStruct((m, d), BF16)],
        compiler_params=_params("parallel"),
        name="oproj",
    )(x, h, w_o, ffn_norm)


def _ffn_up_kernel(x_ref, wg_ref, wu_ref, o_ref):
    x = x_ref[...]
    o_ref[...] = (_silu(_dot(x, wg_ref[...])) * _dot(x, wu_ref[...])).astype(o_ref.dtype)


def _ffn_up_call(x, w_gate, w_up, li, tm, tn):
    m, k = x.shape
    return pl.pallas_call(
        _ffn_up_kernel,
        grid=(D_FF // tn, m // tm),
        in_specs=[pl.BlockSpec((tm, k), lambda n, i: (i, 0)),
                  pl.BlockSpec((None, k, tn), lambda n, i: (li, 0, n)),
                  pl.BlockSpec((None, k, tn), lambda n, i: (li, 0, n))],
        out_specs=pl.BlockSpec((tm, tn), lambda n, i: (i, n)),
        out_shape=jax.ShapeDtypeStruct((m, D_FF), BF16),
        compiler_params=_params("parallel", "parallel"),
        name="ffn_up",
    )(x, w_gate, w_up)


def _ffn_down_kernel(x_ref, h_ref, w_ref, o_ref):
    o_ref[...] = h_ref[...] + _dot(x_ref[...], w_ref[...])


def _ffn_down_call(x, h, w_down, li, tm, tn):
    m, k = x.shape
    return pl.pallas_call(
        _ffn_down_kernel,
        grid=(D_MODEL // tn, m // tm),
        in_specs=[pl.BlockSpec((tm, k), lambda n, i: (i, 0)),
                  pl.BlockSpec((tm, tn), lambda n, i: (i, n)),
                  pl.BlockSpec((None, k, tn), lambda n, i: (li, 0, n))],
        out_specs=pl.BlockSpec((tm, tn), lambda n, i: (i, n)),
        out_shape=jax.ShapeDtypeStruct((m, D_MODEL), F32),
        compiler_params=_params("parallel", "parallel"),
        name="ffn_down",
    )(x, h, w_down)


def _ple_kernel(h_ref, p_ref, wg_ref, wp_ref, g_ref, gnext_ref, h_out, n_out):
    h = h_ref[...]
    hn = _rms(h, g_ref[...]).astype(BF16)
    gate = jax.nn.sigmoid(_dot(hn, wg_ref[...]))
    h = h + gate * _dot(p_ref[...].astype(BF16), wp_ref[...])
    h_out[...] = h
    n_out[...] = _rms(h, gnext_ref[...]).astype(n_out.dtype)


def _ple_call(h, p, w_gate, w_proj, ple_norm, next_norm, li, next_li, next_dtype, tm):
    m, d = h.shape
    rows_per_layer = p.shape[1]
    return pl.pallas_call(
        _ple_kernel,
        grid=(m // tm,),
        in_specs=[pl.BlockSpec((tm, d), lambda i: (i, 0)),
                  pl.BlockSpec((None, tm, PLE_DIM), lambda i: (li, i, 0)),
                  pl.BlockSpec((None, d, d), lambda i: (li, 0, 0)),
                  pl.BlockSpec((None, PLE_DIM, d), lambda i: (li, 0, 0)),
                  pl.BlockSpec((None, 1, d), lambda i: (li, 0, 0)),
                  pl.BlockSpec((None, 1, d), lambda i: (next_li, 0, 0))],
        out_specs=[pl.BlockSpec((tm, d), lambda i: (i, 0)),
                   pl.BlockSpec((tm, d), lambda i: (i, 0))],
        out_shape=[jax.ShapeDtypeStruct((m, d), F32),
                   jax.ShapeDtypeStruct((m, d), next_dtype)],
        compiler_params=_params("parallel"),
        name="ple",
    )(h, p, w_gate, w_proj, ple_norm, next_norm)


def _layer(li, h, xn, p, w, seq, states):
    m = h.shape[0]
    prompt = seq > 1
    tm = 512 if prompt else m
    tm_row = 256 if prompt else m
    hb = GDN_HEADS_PER_STEP

    gates3 = _gates_call(xn, w["ab"], w["a_log"], w["dt_bias"], li, tm)
    z_act = _proj_act_call(xn, w["main"], li, OFF_Z, VAL_DIM, _silu, tm, 1024, "proj_z")
    mix_gates = _proj_act_call(xn, w["rest"], li, REST_G, 2 * D_MODEL, jax.nn.sigmoid, tm, 1024,
                               "proj_mix_gates")

    if prompt:
        qkv, new_qkv = _qkv_prompt_call(xn, w["main"], w["qkv_conv"], li, seq, tm, 512)
        yb, new_sc = _sconv_prompt_call(xn, w["rest"], w["sconv"], li, seq, tm, 512)
        gb = gates3[:, :2 * N_V_HEADS].reshape(m, 2, N_V_HEADS // hb, hb)
        gb = gb.transpose(2, 0, 1, 3).reshape(N_V_HEADS // hb, m, 2 * hb)
        ya, new_gdn = _gdn_prompt_call(qkv, z_act, gb, w["gdn_norm"], li, seq, hb)
    else:
        s_gdn, s_qkv_t, s_sc_t = states
        qkv, new_qkv_t = _qkv_sample_call(xn, w["main"], w["qkv_conv"], s_qkv_t, li, 1024)
        yb, new_sc_t = _sconv_sample_call(xn, w["rest"], w["sconv"], s_sc_t, li, 512)
        new_qkv = new_qkv_t.transpose(1, 0, 2)
        new_sc = new_sc_t.transpose(1, 0, 2)
        gates_flat = gates3[:, N_V_HEADS:].reshape(-1)
        ya, new_gdn = _gdn_sample_call(gates_flat, qkv, z_act, w["gdn_norm"], s_gdn, li, hb,
                                       SAMPLE_TILE)

    mixed = _merge_call(ya, yb, mix_gates, w["gdn_out"], w["sconv_out"], li, tm_row, 1024)
    h2, hn2 = _oproj_call(mixed, h, w["o"], w["ffn_norm"], li, tm_row)
    act = _ffn_up_call(hn2, w["ffn_gate"], w["ffn_up"], li, tm, 512)
    h3 = _ffn_down_call(act, h2, w["ffn_down"], li, tm_row, 1024)
    last = li == DEPTH - 1
    next_norm = w["final_norm"] if last else w["attn_norm"]
    h4, nxt = _ple_call(h3, p, w["ple_gate"], w["ple_proj"], w["ple_norm"], next_norm, li,
                        0 if last else li + 1, F32 if last else BF16, tm_row)
    return h4, nxt, new_gdn, new_qkv, new_sc


def kernel(x_prompt, x_sample, state_gdn, state_qkv_conv, state_short_conv, p_prompt, p_sample, attn_norm, w_in, qkv_conv_w, a_log, dt_bias, gdn_norm, w_gdn_out, sconv_w, w_sconv_out, w_o, ffn_norm, w_ffn_gate, w_ffn_up, w_ffn_down, ple_norm, w_ple_gate, w_ple_proj, final_norm):
    nb, seq, d = x_prompt.shape
    ns = x_sample.shape[0]
    w = {
        "main": w_in[:, :, :OFF_AB].astype(BF16),
        "ab": w_in[:, :, OFF_AB:OFF_REST].astype(BF16),
        "rest": w_in[:, :, OFF_REST:].astype(BF16),
        "qkv_conv": qkv_conv_w,
        "sconv": sconv_w,
        "a_log": a_log[:, None, :],
        "dt_bias": dt_bias[:, None, :],
        "gdn_norm": gdn_norm[:, None, :],
        "gdn_out": w_gdn_out.astype(BF16),
        "sconv_out": w_sconv_out.astype(BF16),
        "o": w_o.astype(BF16),
        "ffn_gate": w_ffn_gate.astype(BF16),
        "ffn_up": w_ffn_up.astype(BF16),
        "ffn_down": w_ffn_down.astype(BF16),
        "ple_gate": w_ple_gate.astype(BF16),
        "ple_proj": w_ple_proj.astype(BF16),
        "attn_norm": attn_norm[:, None, :],
        "ffn_norm": ffn_norm[:, None, :],
        "ple_norm": ple_norm[:, None, :],
        "final_norm": final_norm[None, None, :],
    }
    hp = x_prompt.reshape(nb * seq, d)
    hs = x_sample.reshape(ns, d)
    pp = p_prompt.reshape(DEPTH, nb * seq, PLE_DIM)
    ps = p_sample.reshape(DEPTH, ns, PLE_DIM)
    s_qkv_t = state_qkv_conv.transpose(0, 2, 1, 3)
    s_sc_t = state_short_conv.transpose(0, 2, 1, 3)

    xp = _rmsnorm_call(hp, w["attn_norm"], 0, 256)
    xs = _rmsnorm_call(hs, w["attn_norm"], 0, ns)
    outs_p, outs_s = [], []
    for li in range(DEPTH):
        hp, xp, *st = _layer(li, hp, xp, pp, w, seq, None)
        outs_p.append(st)
        hs, xs, *st = _layer(li, hs, xs, ps, w, 1, (state_gdn, s_qkv_t, s_sc_t))
        outs_s.append(st)
    y_prompt = xp.reshape(nb, seq, d)
    y_sample = xs.reshape(ns, 1, d)
    stack = lambda outs, j: jnp.stack([o[j] for o in outs])
    return (y_prompt, y_sample, stack(outs_p, 0), stack(outs_p, 1), stack(outs_p, 2),
            stack(outs_s, 0), stack(outs_s, 1), stack(outs_s, 2))
```

```python
import functools

import jax
import jax.numpy as jnp
from jax import lax
from jax.experimental import pallas as pl
from jax.experimental.pallas import tpu as pltpu

F32 = jnp.float32
BF16 = jnp.bfloat16
HI = lax.Precision.HIGHEST

D_MODEL = 2048
DEPTH = 4
N_QK_HEADS = 16
N_V_HEADS = 32
HEAD_K = 128
HEAD_V = 128
KEY_DIM = N_QK_HEADS * HEAD_K
VAL_DIM = N_V_HEADS * HEAD_V
QKV_DIM = 2 * KEY_DIM + VAL_DIM
QKV_CONV = 4
SC_CONV = 3
SC_DIM = D_MODEL
CHUNK = 64
D_FF = 5632
PLE_DIM = 256
EPS = 1e-6

OFF_Z = QKV_DIM
OFF_AB = OFF_Z + VAL_DIM
OFF_REST = OFF_AB + 2 * N_V_HEADS
REST_SB, REST_SC, REST_SH, REST_G = 0, SC_DIM, 2 * SC_DIM, 3 * SC_DIM

VMEM_LIMIT_BYTES = 56 * 1024 * 1024
SUBLANES = 8
CONV_SUB_ROWS = 256

GDN_PROMPT_HEADS_PER_STEP = 16
GDN_SAMPLE_HEADS_PER_STEP = 4
SAMPLE_TILE = 8


def _params(*sem):
    return pltpu.CompilerParams(dimension_semantics=sem, vmem_limit_bytes=VMEM_LIMIT_BYTES)


def _dot(a, b):
    return jnp.dot(a, b, preferred_element_type=F32)


def _dot_nt(a, b):
    return lax.dot_general(a, b, (((1,), (1,)), ((), ())), preferred_element_type=F32)


def _dot_tn(a, b, precision=None):
    return lax.dot_general(a, b, (((0,), (0,)), ((), ())), precision=precision,
                           preferred_element_type=F32)


def _dot_hi(a, b):
    return jnp.dot(a, b, precision=HI, preferred_element_type=F32)


def _rms(x, g):
    return x * lax.rsqrt(jnp.mean(x * x, axis=-1, keepdims=True) + EPS) * g


def _silu(x):
    return x * jax.nn.sigmoid(x)


def _rmsnorm_kernel(x_ref, g_ref, o_ref):
    o_ref[...] = _rms(x_ref[...], g_ref[...]).astype(o_ref.dtype)


def _rmsnorm_call(x, gain, li, tm):
    m, d = x.shape
    return pl.pallas_call(
        _rmsnorm_kernel,
        grid=(m // tm,),
        in_specs=[pl.BlockSpec((tm, d), lambda i: (i, 0)),
                  pl.BlockSpec((None, 1, d), lambda i: (li, 0, 0))],
        out_specs=pl.BlockSpec((tm, d), lambda i: (i, 0)),
        out_shape=jax.ShapeDtypeStruct((m, d), BF16),
        compiler_params=_params("parallel"),
        name="rmsnorm",
    )(x, gain)


def _proj_act_kernel(x_ref, w_ref, o_ref, *, act):
    o_ref[...] = act(_dot(x_ref[...], w_ref[...])).astype(o_ref.dtype)


def _proj_act_call(x, w, li, col0, n_out, act, tm, tn, name):
    m, k = x.shape
    cb = col0 // tn
    return pl.pallas_call(
        functools.partial(_proj_act_kernel, act=act),
        grid=(n_out // tn, m // tm),
        in_specs=[pl.BlockSpec((tm, k), lambda n, i: (i, 0)),
                  pl.BlockSpec((None, k, tn), lambda n, i: (li, 0, cb + n))],
        out_specs=pl.BlockSpec((tm, tn), lambda n, i: (i, n)),
        out_shape=jax.ShapeDtypeStruct((m, n_out), F32),
        compiler_params=_params("parallel", "parallel"),
        name=name,
    )(x, w)


def _gates_kernel(x_ref, w_ref, alog_ref, dtb_ref, o_ref):
    r = _dot(x_ref[...], w_ref[...])
    a = r[:, :N_V_HEADS] + dtb_ref[...]
    softplus = jnp.maximum(a, 0.0) + jnp.log(1.0 + jnp.exp(-jnp.abs(a)))
    g = -jnp.exp(alog_ref[...]) * softplus
    o_ref[:, 0:N_V_HEADS] = g
    o_ref[:, N_V_HEADS:2 * N_V_HEADS] = jax.nn.sigmoid(r[:, N_V_HEADS:])
    o_ref[:, 2 * N_V_HEADS:3 * N_V_HEADS] = jnp.exp(g)


def _gates_call(x, w_ab, a_log, dt_bias, li, tm):
    m, k = x.shape
    h2 = 2 * N_V_HEADS
    return pl.pallas_call(
        _gates_kernel,
        grid=(m // tm,),
        in_specs=[pl.BlockSpec((tm, k), lambda i: (i, 0)),
                  pl.BlockSpec((None, k, h2), lambda i: (li, 0, 0)),
                  pl.BlockSpec((None, 1, N_V_HEADS), lambda i: (li, 0, 0)),
                  pl.BlockSpec((None, 1, N_V_HEADS), lambda i: (li, 0, 0))],
        out_specs=pl.BlockSpec((tm, 3 * N_V_HEADS), lambda i: (i, 0)),
        out_shape=jax.ShapeDtypeStruct((m, 3 * N_V_HEADS), F32),
        compiler_params=_params("parallel"),
        name="gates",
    )(x, w_ab, a_log, dt_bias)


def _qkv_prompt_kernel(x_ref, w_ref, cw_ref, act_ref, st_ref, xs_ref, *, tm, tiles_per_seq):
    i = pl.program_id(1)

    @pl.when(i % tiles_per_seq == 0)
    def _():
        xs_ref[0:SUBLANES, :] = jnp.zeros((SUBLANES, xs_ref.shape[1]), F32)

    cw = cw_ref[...]
    starts = list(range(0, tm, CONV_SUB_ROWS))
    x_next = _dot(x_ref[0:CONV_SUB_ROWS, :], w_ref[...])
    for r0 in starts:
        r1 = r0 + CONV_SUB_ROWS
        x = x_next
        if r1 < tm:
            x_next = _dot(x_ref[r1:r1 + CONV_SUB_ROWS, :], w_ref[...])
        xs_ref[SUBLANES + r0:SUBLANES + r1, :] = x
        y = cw[3:4] * x
        for j in range(QKV_CONV - 1):
            lag = QKV_CONV - 1 - j
            y = y + cw[j:j + 1] * xs_ref[SUBLANES + r0 - lag:SUBLANES + r1 - lag, :]
        act_ref[r0:r1, :] = _silu(y)
    st_ref[...] = xs_ref[tm + SUBLANES - (QKV_CONV - 1):tm + SUBLANES, :]
    xs_ref[0:SUBLANES, :] = xs_ref[tm:tm + SUBLANES, :]


def _qkv_prompt_call(x, w, conv_w, li, seq, tm, tn):
    m, k = x.shape
    tps = seq // tm
    return pl.pallas_call(
        functools.partial(_qkv_prompt_kernel, tm=tm, tiles_per_seq=tps),
        grid=(QKV_DIM // tn, m // tm),
        in_specs=[pl.BlockSpec((tm, k), lambda n, i: (i, 0)),
                  pl.BlockSpec((None, k, tn), lambda n, i: (li, 0, n)),
                  pl.BlockSpec((None, QKV_CONV, tn), lambda n, i: (li, 0, n))],
        out_specs=[pl.BlockSpec((tm, tn), lambda n, i: (i, n)),
                   pl.BlockSpec((None, QKV_CONV - 1, tn), lambda n, i: (i // tps, 0, n))],
        out_shape=[jax.ShapeDtypeStruct((m, QKV_DIM), F32),
                   jax.ShapeDtypeStruct((m // seq, QKV_CONV - 1, QKV_DIM), F32)],
        scratch_shapes=[pltpu.VMEM((tm + SUBLANES, tn), F32)],
        compiler_params=_params("arbitrary", "arbitrary"),
        name="qkv_conv_prompt",
    )(x, w, conv_w)


def _qkv_sample_kernel(x_ref, w_ref, cw_ref, s_ref, act_ref, st_ref):
    x = _dot(x_ref[...], w_ref[...])
    cw = cw_ref[...]
    y = cw[3:4] * x
    for j in range(QKV_CONV - 1):
        y = y + cw[j:j + 1] * s_ref[j]
    act_ref[...] = _silu(y)
    st_ref[0] = s_ref[1]
    st_ref[1] = s_ref[2]
    st_ref[2] = x


def _qkv_sample_call(x, w, conv_w, state_t, li, tn):
    m, k = x.shape
    return pl.pallas_call(
        _qkv_sample_kernel,
        grid=(QKV_DIM // tn,),
        in_specs=[pl.BlockSpec((m, k), lambda n: (0, 0)),
                  pl.BlockSpec((None, k, tn), lambda n: (li, 0, n)),
                  pl.BlockSpec((None, QKV_CONV, tn), lambda n: (li, 0, n)),
                  pl.BlockSpec((None, QKV_CONV - 1, m, tn), lambda n: (li, 0, 0, n))],
        out_specs=[pl.BlockSpec((m, tn), lambda n: (0, n)),
                   pl.BlockSpec((QKV_CONV - 1, m, tn), lambda n: (0, 0, n))],
        out_shape=[jax.ShapeDtypeStruct((m, QKV_DIM), F32),
                   jax.ShapeDtypeStruct((QKV_CONV - 1, m, QKV_DIM), F32)],
        compiler_params=_params("parallel"),
        name="qkv_conv_sample",
    )(x, w, conv_w, state_t)


def _sconv_prompt_kernel(x_ref, wb_ref, wc_ref, wh_ref, cw_ref, y_ref, st_ref, us_ref, *,
                         tm, tiles_per_seq):
    i = pl.program_id(1)

    @pl.when(i % tiles_per_seq == 0)
    def _():
        us_ref[0:SUBLANES, :] = jnp.zeros((SUBLANES, us_ref.shape[1]), F32)

    cw = cw_ref[...]
    x = x_ref[...]
    u = _dot(x, wc_ref[...]) * _dot(x, wh_ref[...])
    us_ref[SUBLANES:tm + SUBLANES, :] = u
    conv = cw[2:3] * u
    for j in range(SC_CONV - 1):
        lag = SC_CONV - 1 - j
        conv = conv + cw[j:j + 1] * us_ref[SUBLANES - lag:tm + SUBLANES - lag, :]
    y_ref[...] = (_dot(x, wb_ref[...]) * conv).astype(y_ref.dtype)
    st_ref[...] = us_ref[tm + SUBLANES - (SC_CONV - 1):tm + SUBLANES, :]
    us_ref[0:SUBLANES, :] = us_ref[tm:tm + SUBLANES, :]


def _sconv_prompt_call(x, w_rest, conv_w, li, seq, tm, tn):
    m, k = x.shape
    tps = seq // tm
    cb, cc, ch = REST_SB // tn, REST_SC // tn, REST_SH // tn
    return pl.pallas_call(
        functools.partial(_sconv_prompt_kernel, tm=tm, tiles_per_seq=tps),
        grid=(SC_DIM // tn, m // tm),
        in_specs=[pl.BlockSpec((tm, k), lambda n, i: (i, 0)),
                  pl.BlockSpec((None, k, tn), lambda n, i: (li, 0, cb + n)),
                  pl.BlockSpec((None, k, tn), lambda n, i: (li, 0, cc + n)),
                  pl.BlockSpec((None, k, tn), lambda n, i: (li, 0, ch + n)),
                  pl.BlockSpec((None, SC_CONV, tn), lambda n, i: (li, 0, n))],
        out_specs=[pl.BlockSpec((tm, tn), lambda n, i: (i, n)),
                   pl.BlockSpec((None, SC_CONV - 1, tn), lambda n, i: (i // tps, 0, n))],
        out_shape=[jax.ShapeDtypeStruct((m, SC_DIM), BF16),
                   jax.ShapeDtypeStruct((m // seq, SC_CONV - 1, SC_DIM), F32)],
        scratch_shapes=[pltpu.VMEM((tm + SUBLANES, tn), F32)],
        compiler_params=_params("arbitrary", "arbitrary"),
        name="sconv_prompt",
    )(x, w_rest, w_rest, w_rest, conv_w)


def _sconv_sample_kernel(x_ref, wb_ref, wc_ref, wh_ref, cw_ref, s_ref, y_ref, st_ref):
    x = x_ref[...]
    u = _dot(x, wc_ref[...]) * _dot(x, wh_ref[...])
    cw = cw_ref[...]
    conv = cw[2:3] * u
    for j in range(SC_CONV - 1):
        conv = conv + cw[j:j + 1] * s_ref[j]
    y_ref[...] = (_dot(x, wb_ref[...]) * conv).astype(y_ref.dtype)
    st_ref[0] = s_ref[1]
    st_ref[1] = u


def _sconv_sample_call(x, w_rest, conv_w, state_t, li, tn):
    m, k = x.shape
    cb, cc, ch = REST_SB // tn, REST_SC // tn, REST_SH // tn
    return pl.pallas_call(
        _sconv_sample_kernel,
        grid=(SC_DIM // tn,),
        in_specs=[pl.BlockSpec((m, k), lambda n: (0, 0)),
                  pl.BlockSpec((None, k, tn), lambda n: (li, 0, cb + n)),
                  pl.BlockSpec((None, k, tn), lambda n: (li, 0, cc + n)),
                  pl.BlockSpec((None, k, tn), lambda n: (li, 0, ch + n)),
                  pl.BlockSpec((None, SC_CONV, tn), lambda n: (li, 0, n)),
                  pl.BlockSpec((None, SC_CONV - 1, m, tn), lambda n: (li, 0, 0, n))],
        out_specs=[pl.BlockSpec((m, tn), lambda n: (0, n)),
                   pl.BlockSpec((SC_CONV - 1, m, tn), lambda n: (0, 0, n))],
        out_shape=[jax.ShapeDtypeStruct((m, SC_DIM), BF16),
                   jax.ShapeDtypeStruct((SC_CONV - 1, m, SC_DIM), F32)],
        compiler_params=_params("parallel"),
        name="sconv_sample",
    )(x, w_rest, w_rest, w_rest, conv_w, state_t)


def _gdn_prompt_kernel(q_ref, k_ref, v_ref, z_ref, gp_ref, gn_ref, y_ref, s_ref, *, hb):
    c = CHUNK
    c2 = 2 * c
    npairs = hb // 2

    @pl.when(pl.program_id(2) == 0)
    def _():
        s_ref[...] = jnp.zeros(s_ref.shape, F32)

    ri = lax.broadcasted_iota(jnp.int32, (c2, c2), 0)
    ci = lax.broadcasted_iota(jnp.int32, (c2, c2), 1)
    r_in_rblk = (ri & (c - 1)).astype(jnp.uint32)
    c_in_rblk = (ci - (ri - (ri & (c - 1)))).astype(jnp.uint32)
    r_in_cblk = (ri - (ci - (ci & (c - 1)))).astype(jnp.uint32)
    c_in_cblk = (ci & (c - 1)).astype(jnp.uint32)
    lower = c_in_rblk <= r_in_rblk
    strict = c_in_rblk < r_in_rblk
    same = c_in_rblk < jnp.uint32(c)
    upper = r_in_cblk <= c_in_cblk

    gp = gp_ref[...]
    g = gp[:, :npairs]
    beta = gp[:, npairs:]
    g_col = _dot_hi(lower.astype(F32), g)
    g_row = _dot_tn(g, upper.astype(F32), precision=HI)
    g_last = _dot_hi(same.astype(F32), g)
    e_g = jnp.exp(g_col)
    e_rest = jnp.exp(g_last - g_col)
    e_last = jnp.exp(g_last)
    beta_e_g = beta * e_g
    gn = gn_ref[...]

    pairs = range(npairs)
    heads = [(p, r) for p in pairs for r in range(2)]
    k_st, q_st, k_b, kq = [], [], [], []
    for p in pairs:
        q = q_ref[:, p * HEAD_K:(p + 1) * HEAD_K]
        k = k_ref[:, p * HEAD_K:(p + 1) * HEAD_K]
        qn = q * (lax.rsqrt(jnp.sum(q * q, axis=-1, keepdims=True) + EPS) * (HEAD_K ** -0.5))
        kn = k * lax.rsqrt(jnp.sum(k * k, axis=-1, keepdims=True) + EPS)
        k_st.append(jnp.concatenate([kn, kn], axis=0))
        q_st.append(jnp.concatenate([qn, qn], axis=0))
        k_b.append(k_st[p].astype(BF16))
        kq.append(_dot_nt(jnp.concatenate([k_b[p], q_st[p].astype(BF16)], axis=0), k_b[p]))
    decay = [jnp.exp(jnp.where(lower, g_col[:, p:p + 1] - g_row[p:p + 1, :], -jnp.inf)) for p in pairs]
    a = [jnp.where(strict, beta[:, p:p + 1] * kq[p][:c2] * decay[p], 0.0) for p in pairs]
    qkd_b = [(kq[p][c2:] * decay[p]).astype(BF16) for p in pairs]
    a_b = [x.astype(BF16) for x in a]
    corr = [-x for x in a]
    pw = [_dot(x, x) for x in a_b]
    span = 2
    while span < c:
        pw_b = [x.astype(BF16) for x in pw]
        corr = [corr[p] + pw[p] + _dot(corr[p].astype(BF16), pw_b[p]) for p in pairs]
        span *= 2
        if span < c:
            pw = [_dot(x, x) for x in pw_b]
    sol = []
    for p in pairs:
        v_st = jnp.concatenate([v_ref[:, (2 * p + r) * HEAD_V:(2 * p + r + 1) * HEAD_V]
                                for r in range(2)], axis=0)
        rhs = jnp.concatenate([v_st * beta[:, p:p + 1], k_st[p] * beta_e_g[:, p:p + 1]], axis=1)
        sol.append(rhs + _dot(corr[p].astype(BF16), rhs.astype(BF16)))
    qe_st = [q_st[p] * e_g[:, p:p + 1] for p in pairs]
    kd_b = [(k_st[p] * e_rest[:, p:p + 1]).astype(BF16) for p in pairs]
    s_old = [s_ref[2 * p + r] for p, r in heads]
    ws_qs = [_dot(jnp.concatenate([sol[p][r * c:(r + 1) * c, HEAD_V:], qe_st[p][r * c:(r + 1) * c]],
                                  axis=0).astype(BF16), s_old[2 * p + r].astype(BF16))
             for p, r in heads]
    u_new = [sol[p][r * c:(r + 1) * c, :HEAD_V] - ws_qs[2 * p + r][:c] for p, r in heads]
    u_new_b = [x.astype(BF16) for x in u_new]
    for p, r in heads:
        h = 2 * p + r
        s_ref[h] = (s_old[h] * e_last[r * c:r * c + 1, p:p + 1]
                    + _dot_tn(kd_b[p][r * c:(r + 1) * c], u_new_b[h]))
    for p in pairs:
        o_st = (jnp.concatenate([ws_qs[2 * p][c:], ws_qs[2 * p + 1][c:]], axis=0)
                + _dot(qkd_b[p], jnp.concatenate([u_new_b[2 * p], u_new_b[2 * p + 1]], axis=0)))
        y_st = _rms(o_st, gn)
        for r in range(2):
            cols = slice((2 * p + r) * HEAD_V, (2 * p + r + 1) * HEAD_V)
            y_ref[:, cols] = (y_st[r * c:(r + 1) * c] * z_ref[:, cols]).astype(y_ref.dtype)


def _gate_pair_layout(x, hb):
    rows, heads = x.shape
    x = x.reshape(rows // CHUNK, CHUNK, heads // hb, hb // 2, 2)
    return x.transpose(2, 0, 4, 1, 3).reshape(heads // hb, rows // CHUNK, 2 * CHUNK, hb // 2)


def _gdn_prompt_call(qkv, z_act, gate_pairs, gdn_norm, li, seq, hb):
    m = qkv.shape[0]
    nb, nc = m // seq, seq // CHUNK
    qw, vw = HEAD_K * hb // 2, HEAD_V * hb
    kb, vb = KEY_DIM // qw, 2 * KEY_DIM // vw
    return pl.pallas_call(
        functools.partial(_gdn_prompt_kernel, hb=hb),
        grid=(nb, N_V_HEADS // hb, nc),
        in_specs=[pl.BlockSpec((CHUNK, qw), lambda b, h, c: (b * nc + c, h)),
                  pl.BlockSpec((CHUNK, qw), lambda b, h, c: (b * nc + c, kb + h)),
                  pl.BlockSpec((CHUNK, vw), lambda b, h, c: (b * nc + c, vb + h)),
                  pl.BlockSpec((CHUNK, vw), lambda b, h, c: (b * nc + c, h)),
                  pl.BlockSpec((None, None, 2 * CHUNK, hb), lambda b, h, c: (h, b * nc + c, 0, 0)),
                  pl.BlockSpec((None, 1, HEAD_V), lambda b, h, c: (li, 0, 0))],
        out_specs=[pl.BlockSpec((CHUNK, vw), lambda b, h, c: (b * nc + c, h)),
                   pl.BlockSpec((None, hb, HEAD_K, HEAD_V), lambda b, h, c: (b, h, 0, 0))],
        out_shape=[jax.ShapeDtypeStruct((m, VAL_DIM), BF16),
                   jax.ShapeDtypeStruct((nb, N_V_HEADS, HEAD_K, HEAD_V), F32)],
        compiler_params=_params("parallel", "parallel", "arbitrary"),
        name="gdn_prompt",
    )(qkv, qkv, qkv, z_act, gate_pairs, gdn_norm)


def _lanes_to_sublanes(row):
    n = row.shape[1]
    ri = lax.broadcasted_iota(jnp.int32, (n, n), 0)
    ci = lax.broadcasted_iota(jnp.int32, (n, n), 1)
    diag = jnp.where(ri == ci, jnp.broadcast_to(row, (n, n)), 0.0)
    return jnp.broadcast_to(jnp.sum(diag, axis=1, keepdims=True), (n, n))


def _gdn_sample_kernel(gate_ref, q_ref, k_ref, v_ref, z_ref, gn_ref, s_ref, y_ref, so_ref, *, hb, bt):
    hblk = pl.program_id(0)
    tile = pl.program_id(1)
    gn = gn_ref[...]
    for b in range(bt):
        base = (tile * bt + b) * (2 * N_V_HEADS) + hblk * hb
        for j in range(hb // 2):
            q = q_ref[b:b + 1, j * HEAD_K:(j + 1) * HEAD_K]
            k = k_ref[b:b + 1, j * HEAD_K:(j + 1) * HEAD_K]
            qn = q * (lax.rsqrt(jnp.sum(q * q, axis=-1, keepdims=True) + EPS) * (HEAD_K ** -0.5))
            kn = k * lax.rsqrt(jnp.sum(k * k, axis=-1, keepdims=True) + EPS)
            q_col = _lanes_to_sublanes(qn)
            k_col = _lanes_to_sublanes(kn)
            for r in range(2):
                h = 2 * j + r
                beta = gate_ref[base + h]
                e_g = gate_ref[base + N_V_HEADS + h]
                s = s_ref[b, h]
                s_k = jnp.sum(s * k_col, axis=0, keepdims=True)
                delta = beta * (v_ref[b:b + 1, h * HEAD_V:(h + 1) * HEAD_V] - e_g * s_k)
                s_new = e_g * s + k_col * delta
                so_ref[b, h] = s_new
                o = jnp.sum(s_new * q_col, axis=0, keepdims=True)
                y = _rms(o, gn) * z_ref[b:b + 1, h * HEAD_V:(h + 1) * HEAD_V]
                y_ref[b:b + 1, h * HEAD_V:(h + 1) * HEAD_V] = y


def _gdn_sample_call(gates_flat, qkv, z_act, gdn_norm, state, li, hb, bt):
    m = qkv.shape[0]
    qw, vw = HEAD_K * hb // 2, HEAD_V * hb
    kb, vb = KEY_DIM // qw, 2 * KEY_DIM // vw
    return pl.pallas_call(
        functools.partial(_gdn_sample_kernel, hb=hb, bt=bt),
        grid=(N_V_HEADS // hb, m // bt),
        in_specs=[pl.BlockSpec(memory_space=pltpu.SMEM),
                  pl.BlockSpec((bt, qw), lambda h, t: (t, h)),
                  pl.BlockSpec((bt, qw), lambda h, t: (t, kb + h)),
                  pl.BlockSpec((bt, vw), lambda h, t: (t, vb + h)),
                  pl.BlockSpec((bt, vw), lambda h, t: (t, h)),
                  pl.BlockSpec((None, 1, HEAD_V), lambda h, t: (li, 0, 0)),
                  pl.BlockSpec((None, bt, hb, HEAD_K, HEAD_V), lambda h, t: (li, t, h, 0, 0))],
        out_specs=[pl.BlockSpec((bt, vw), lambda h, t: (t, h)),
                   pl.BlockSpec((bt, hb, HEAD_K, HEAD_V), lambda h, t: (t, h, 0, 0))],
        out_shape=[jax.ShapeDtypeStruct((m, VAL_DIM), F32),
                   jax.ShapeDtypeStruct((m, N_V_HEADS, HEAD_K, HEAD_V), F32)],
        compiler_params=_params("parallel", "parallel"),
        name="gdn_sample",
    )(gates_flat, qkv, qkv, qkv, z_act, gdn_norm, state)


def _merge_kernel(ya_ref, yb_ref, ga_ref, gb_ref, wa_ref, wb_ref, o_ref):
    a = _dot(ya_ref[...].astype(BF16), wa_ref[...])
    b = _dot(yb_ref[...], wb_ref[...])
    o_ref[...] = (ga_ref[...] * a + gb_ref[...] * b).astype(o_ref.dtype)


def _merge_call(ya, yb, gates, w_gdn_out, w_sconv_out, li, tm, tn):
    m = ya.shape[0]
    gbo = D_MODEL // tn
    return pl.pallas_call(
        _merge_kernel,
        grid=(D_MODEL // tn, m // tm),
        in_specs=[pl.BlockSpec((tm, VAL_DIM), lambda n, i: (i, 0)),
                  pl.BlockSpec((tm, SC_DIM), lambda n, i: (i, 0)),
                  pl.BlockSpec((tm, tn), lambda n, i: (i, n)),
                  pl.BlockSpec((tm, tn), lambda n, i: (i, gbo + n)),
                  pl.BlockSpec((None, VAL_DIM, tn), lambda n, i: (li, 0, n)),
                  pl.BlockSpec((None, SC_DIM, tn), lambda n, i: (li, 0, n))],
        out_specs=pl.BlockSpec((tm, tn), lambda n, i: (i, n)),
        out_shape=jax.ShapeDtypeStruct((m, D_MODEL), BF16),
        compiler_params=_params("parallel", "parallel"),
        name="merge",
    )(ya, yb, gates, gates, w_gdn_out, w_sconv_out)


def _oproj_kernel(x_ref, h_ref, w_ref, g_ref, h_out, n_out):
    h = h_ref[...] + _dot(x_ref[...], w_ref[...])
    h_out[...] = h
    n_out[...] = _rms(h, g_ref[...]).astype(n_out.dtype)


def _oproj_call(x, h, w_o, ffn_norm, li, tm):
    m, d = h.shape
    return pl.pallas_call(
        _oproj_kernel,
        grid=(m // tm,),
        in_specs=[pl.BlockSpec((tm, d), lambda i: (i, 0)),
                  pl.BlockSpec((tm, d), lambda i: (i, 0)),
                  pl.BlockSpec((None, d, d), lambda i: (li, 0, 0)),
                  pl.BlockSpec((None, 1, d), lambda i: (li, 0, 0))],
        out_specs=[pl.BlockSpec((tm, d), lambda i: (i, 0)),
                   pl.BlockSpec((tm, d), lambda i: (i, 0))],
        out_shape=[jax.ShapeDtypeStruct((m, d), F32),
                   jax.ShapeDtypeStruct((m, d), BF16)],
        compiler_params=_params("parallel"),
        name="oproj",
    )(x, h, w_o, ffn_norm)


def _ffn_up_kernel(x_ref, wg_ref, wu_ref, o_ref):
    x = x_ref[...]
    o_ref[...] = (_silu(_dot(x, wg_ref[...])) * _dot(x, wu_ref[...])).astype(o_ref.dtype)


def _ffn_up_call(x, w_gate, w_up, li, tm, tn):
    m, k = x.shape
    return pl.pallas_call(
        _ffn_up_kernel,
        grid=(D_FF // tn, m // tm),
        in_specs=[pl.BlockSpec((tm, k), lambda n, i: (i, 0)),
                  pl.BlockSpec((None, k, tn), lambda n, i: (li, 0, n)),
                  pl.BlockSpec((None, k, tn), lambda n, i: (li, 0, n))],
        out_specs=pl.BlockSpec((tm, tn), lambda n, i: (i, n)),
        out_shape=jax.ShapeDtypeStruct((m, D_FF), BF16),
        compiler_params=_params("parallel", "parallel"),
        name="ffn_up",
    )(x, w_gate, w_up)


def _ffn_down_kernel(x_ref, h_ref, w_ref, o_ref):
    o_ref[...] = h_ref[...] + _dot(x_ref[...], w_ref[...])


def _ffn_down_call(x, h, w_down, li, tm, tn):
    m, k = x.shape
    return pl.pallas_call(
        _ffn_down_kernel,
        grid=(D_MODEL // tn, m // tm),
        in_specs=[pl.BlockSpec((tm, k), lambda n, i: (i, 0)),
                  pl.BlockSpec((tm, tn), lambda n, i: (i, n)),
                  pl.BlockSpec((None, k, tn), lambda n, i: (li, 0, n))],
        out_specs=pl.BlockSpec((tm, tn), lambda n, i: (i, n)),
        out_shape=jax.ShapeDtypeStruct((m, D_MODEL), F32),
        compiler_params=_params("parallel", "parallel"),
        name="ffn_down",
    )(x, h, w_down)


def _ple_kernel(h_ref, p_ref, wg_ref, wp_ref, g_ref, gnext_ref, h_out, n_out):
    h = h_ref[...]
    hn = _rms(h, g_ref[...]).astype(BF16)
    gate = jax.nn.sigmoid(_dot(hn, wg_ref[...]))
    h = h + gate * _dot(p_ref[...].astype(BF16), wp_ref[...])
    h_out[...] = h
    n_out[...] = _rms(h, gnext_ref[...]).astype(n_out.dtype)


def _ple_call(h, p, w_gate, w_proj, ple_norm, next_norm, li, next_li, next_dtype, tm):
    m, d = h.shape
    return pl.pallas_call(
        _ple_kernel,
        grid=(m // tm,),
        in_specs=[pl.BlockSpec((tm, d), lambda i: (i, 0)),
                  pl.BlockSpec((None, tm, PLE_DIM), lambda i: (li, i, 0)),
                  pl.BlockSpec((None, d, d), lambda i: (li, 0, 0)),
                  pl.BlockSpec((None, PLE_DIM, d), lambda i: (li, 0, 0)),
                  pl.BlockSpec((None, 1, d), lambda i: (li, 0, 0)),
                  pl.BlockSpec((None, 1, d), lambda i: (next_li, 0, 0))],
        out_specs=[pl.BlockSpec((tm, d), lambda i: (i, 0)),
                   pl.BlockSpec((tm, d), lambda i: (i, 0))],
        out_shape=[jax.ShapeDtypeStruct((m, d), F32),
                   jax.ShapeDtypeStruct((m, d), next_dtype)],
        compiler_params=_params("parallel"),
        name="ple",
    )(h, p, w_gate, w_proj, ple_norm, next_norm)


def _layer(li, h, xn, p, w, seq, states):
    m = h.shape[0]
    prompt = seq > 1
    tm = 512 if prompt else m
    tm_row = 256 if prompt else m
    hb = GDN_PROMPT_HEADS_PER_STEP if prompt else GDN_SAMPLE_HEADS_PER_STEP

    gates3 = _gates_call(xn, w["ab"], w["a_log"], w["dt_bias"], li, tm)
    z_act = _proj_act_call(xn, w["main"], li, OFF_Z, VAL_DIM, _silu, tm, 1024, "proj_z")
    mix_gates = _proj_act_call(xn, w["rest"], li, REST_G, 2 * D_MODEL, jax.nn.sigmoid, tm, 1024,
                               "proj_mix_gates")

    if prompt:
        qkv, new_qkv = _qkv_prompt_call(xn, w["main"], w["qkv_conv"], li, seq, tm, 512)
        yb, new_sc = _sconv_prompt_call(xn, w["rest"], w["sconv"], li, seq, tm, 512)
        gate_pairs = jnp.concatenate(
            [_gate_pair_layout(gates3[:, :N_V_HEADS], hb),
             _gate_pair_layout(gates3[:, N_V_HEADS:2 * N_V_HEADS], hb)], axis=-1)
        ya, new_gdn = _gdn_prompt_call(qkv, z_act, gate_pairs, w["gdn_norm"], li, seq, hb)
    else:
        s_gdn, s_qkv_t, s_sc_t = states
        qkv, new_qkv_t = _qkv_sample_call(xn, w["main"], w["qkv_conv"], s_qkv_t, li, 1024)
        yb, new_sc_t = _sconv_sample_call(xn, w["rest"], w["sconv"], s_sc_t, li, 512)
        new_qkv = new_qkv_t.transpose(1, 0, 2)
        new_sc = new_sc_t.transpose(1, 0, 2)
        gates_flat = gates3[:, N_V_HEADS:].reshape(-1)
        ya, new_gdn = _gdn_sample_call(gates_flat, qkv, z_act, w["gdn_norm"], s_gdn, li, hb,
                                       SAMPLE_TILE)

    mixed = _merge_call(ya, yb, mix_gates, w["gdn_out"], w["sconv_out"], li, tm_row, 1024)
    h2, hn2 = _oproj_call(mixed, h, w["o"], w["ffn_norm"], li, tm_row)
    act = _ffn_up_call(hn2, w["ffn_gate"], w["ffn_up"], li, tm, 512)
    h3 = _ffn_down_call(act, h2, w["ffn_down"], li, tm_row, 1024)
    last = li == DEPTH - 1
    next_norm = w["final_norm"] if last else w["attn_norm"]
    h4, nxt = _ple_call(h3, p, w["ple_gate"], w["ple_proj"], w["ple_norm"], next_norm, li,
                        0 if last else li + 1, F32 if last else BF16, tm_row)
    return h4, nxt, new_gdn, new_qkv, new_sc


def kernel(x_prompt, x_sample, state_gdn, state_qkv_conv, state_short_conv, p_prompt, p_sample, attn_norm, w_in, qkv_conv_w, a_log, dt_bias, gdn_norm, w_gdn_out, sconv_w, w_sconv_out, w_o, ffn_norm, w_ffn_gate, w_ffn_up, w_ffn_down, ple_norm, w_ple_gate, w_ple_proj, final_norm):
    nb, seq, d = x_prompt.shape
    ns = x_sample.shape[0]
    w_in_b = w_in.astype(BF16)
    w = {
        "main": w_in_b,
        "ab": w_in_b[:, :, OFF_AB:OFF_REST],
        "rest": w_in_b[:, :, OFF_REST:],
        "qkv_conv": qkv_conv_w,
        "sconv": sconv_w,
        "a_log": a_log[:, None, :],
        "dt_bias": dt_bias[:, None, :],
        "gdn_norm": gdn_norm[:, None, :],
        "gdn_out": w_gdn_out.astype(BF16),
        "sconv_out": w_sconv_out.astype(BF16),
        "o": w_o.astype(BF16),
        "ffn_gate": w_ffn_gate.astype(BF16),
        "ffn_up": w_ffn_up.astype(BF16),
        "ffn_down": w_ffn_down.astype(BF16),
        "ple_gate": w_ple_gate.astype(BF16),
        "ple_proj": w_ple_proj.astype(BF16),
        "attn_norm": attn_norm[:, None, :],
        "ffn_norm": ffn_norm[:, None, :],
        "ple_norm": ple_norm[:, None, :],
        "final_norm": final_norm[None, None, :],
    }
    hp = x_prompt.reshape(nb * seq, d)
    hs = x_sample.reshape(ns, d)
    pp = p_prompt.reshape(DEPTH, nb * seq, PLE_DIM)
    ps = p_sample.reshape(DEPTH, ns, PLE_DIM)
    s_qkv_t = state_qkv_conv.transpose(0, 2, 1, 3)
    s_sc_t = state_short_conv.transpose(0, 2, 1, 3)

    xp = _rmsnorm_call(hp, w["attn_norm"], 0, 256)
    xs = _rmsnorm_call(hs, w["attn_norm"], 0, ns)
    outs_p, outs_s = [], []
    for li in range(DEPTH):
        hp, xp, *st = _layer(li, hp, xp, pp, w, seq, None)
        outs_p.append(st)
        hs, xs, *st = _layer(li, hs, xs, ps, w, 1, (state_gdn, s_qkv_t, s_sc_t))
        outs_s.append(st)
    y_prompt = xp.reshape(nb, seq, d)
    y_sample = xs.reshape(ns, 1, d)
    stack = lambda outs, j: jnp.stack([o[j] for o in outs])
    return (y_prompt, y_sample, stack(outs_p, 0), stack(outs_p, 1), stack(outs_p, 2),
            stack(outs_s, 0), stack(outs_s, 1), stack(outs_s, 2))
```

```python
import functools

import jax
import jax.numpy as jnp
from jax import lax
from jax.experimental import pallas as pl
from jax.experimental.pallas import tpu as pltpu

F32 = jnp.float32
BF16 = jnp.bfloat16
HI = lax.Precision.HIGHEST

D_MODEL = 2048
DEPTH = 4
N_QK_HEADS = 16
N_V_HEADS = 32
HEAD_K = 128
HEAD_V = 128
KEY_DIM = N_QK_HEADS * HEAD_K
VAL_DIM = N_V_HEADS * HEAD_V
QKV_DIM = 2 * KEY_DIM + VAL_DIM
QKV_CONV = 4
SC_CONV = 3
SC_DIM = D_MODEL
CHUNK = 64
D_FF = 5632
PLE_DIM = 256
EPS = 1e-6

OFF_Z = QKV_DIM
OFF_AB = OFF_Z + VAL_DIM
OFF_REST = OFF_AB + 2 * N_V_HEADS
REST_SB, REST_SC, REST_SH, REST_G = 0, SC_DIM, 2 * SC_DIM, 3 * SC_DIM

VMEM_LIMIT_BYTES = 56 * 1024 * 1024
SUBLANES = 8
LANES = 128
CONV_SUB_ROWS = 256

GDN_PROMPT_HEADS_PER_STEP = 16
GDN_SAMPLE_HEADS_PER_STEP = 4
SAMPLE_TILE = 8


def _params(*sem):
    return pltpu.CompilerParams(dimension_semantics=sem, vmem_limit_bytes=VMEM_LIMIT_BYTES)


def _dot(a, b):
    return jnp.dot(a, b, preferred_element_type=F32)


def _dot_nt(a, b):
    return lax.dot_general(a, b, (((1,), (1,)), ((), ())), preferred_element_type=F32)


def _dot_tn(a, b, precision=None):
    return lax.dot_general(a, b, (((0,), (0,)), ((), ())), precision=precision,
                           preferred_element_type=F32)


def _dot_hi(a, b):
    return jnp.dot(a, b, precision=HI, preferred_element_type=F32)


def _rms(x, g):
    return x * lax.rsqrt(jnp.mean(x * x, axis=-1, keepdims=True) + EPS) * g


def _silu(x):
    return x * jax.nn.sigmoid(x)


def _rmsnorm_kernel(x_ref, g_ref, o_ref):
    o_ref[...] = _rms(x_ref[...], g_ref[...]).astype(o_ref.dtype)


def _rmsnorm_call(x, gain, li, tm):
    m, d = x.shape
    return pl.pallas_call(
        _rmsnorm_kernel,
        grid=(m // tm,),
        in_specs=[pl.BlockSpec((tm, d), lambda i: (i, 0)),
                  pl.BlockSpec((None, 1, d), lambda i: (li, 0, 0))],
        out_specs=pl.BlockSpec((tm, d), lambda i: (i, 0)),
        out_shape=jax.ShapeDtypeStruct((m, d), BF16),
        compiler_params=_params("parallel"),
        name="rmsnorm",
    )(x, gain)


def _cast_kernel(a_ref, o_ref):
    o_ref[...] = a_ref[...].astype(o_ref.dtype)


def _cast_shift_kernel(a_ref, b_ref, o_ref, *, shift):
    o_ref[...] = jnp.concatenate([a_ref[:, shift:], b_ref[:, :shift]], axis=1).astype(o_ref.dtype)


def _cast_columns_call(w, col0, width, tk, tn):
    depth, k, _ = w.shape
    shift = col0 % LANES
    base = (col0 - shift) // tn
    out_spec = pl.BlockSpec((None, tk, tn), lambda l, r, c: (l, r, c))
    a_spec = pl.BlockSpec((None, tk, tn), lambda l, r, c: (l, r, base + c))
    common = dict(grid=(depth, k // tk, width // tn), out_specs=out_spec,
                  out_shape=jax.ShapeDtypeStruct((depth, k, width), BF16),
                  compiler_params=_params("parallel", "parallel", "parallel"))
    if shift == 0:
        return pl.pallas_call(_cast_kernel, in_specs=[a_spec], name="cast_w", **common)(w)
    lane_blocks = tn // LANES
    b_spec = pl.BlockSpec((None, tk, LANES), lambda l, r, c: (l, r, (base + c + 1) * lane_blocks))
    return pl.pallas_call(functools.partial(_cast_shift_kernel, shift=shift),
                          in_specs=[a_spec, b_spec], name="cast_shift_w", **common)(w, w)


def _proj_act_kernel(x_ref, w_ref, o_ref, *, act):
    o_ref[...] = act(_dot(x_ref[...], w_ref[...])).astype(o_ref.dtype)


def _proj_act_call(x, w, li, col0, n_out, act, tm, tn, name):
    m, k = x.shape
    cb = col0 // tn
    return pl.pallas_call(
        functools.partial(_proj_act_kernel, act=act),
        grid=(n_out // tn, m // tm),
        in_specs=[pl.BlockSpec((tm, k), lambda n, i: (i, 0)),
                  pl.BlockSpec((None, k, tn), lambda n, i: (li, 0, cb + n))],
        out_specs=pl.BlockSpec((tm, tn), lambda n, i: (i, n)),
        out_shape=jax.ShapeDtypeStruct((m, n_out), F32),
        compiler_params=_params("parallel", "parallel"),
        name=name,
    )(x, w)


def _gates_kernel(x_ref, w_ref, alog_ref, dtb_ref, o_ref):
    r = _dot(x_ref[...], w_ref[...])
    a = r[:, :N_V_HEADS] + dtb_ref[...]
    softplus = jnp.maximum(a, 0.0) + jnp.log(1.0 + jnp.exp(-jnp.abs(a)))
    g = -jnp.exp(alog_ref[...]) * softplus
    o_ref[:, 0:N_V_HEADS] = g
    o_ref[:, N_V_HEADS:2 * N_V_HEADS] = jax.nn.sigmoid(r[:, N_V_HEADS:])
    o_ref[:, 2 * N_V_HEADS:3 * N_V_HEADS] = jnp.exp(g)


def _gates_call(x, w_ab, a_log, dt_bias, li, tm):
    m, k = x.shape
    h2 = 2 * N_V_HEADS
    return pl.pallas_call(
        _gates_kernel,
        grid=(m // tm,),
        in_specs=[pl.BlockSpec((tm, k), lambda i: (i, 0)),
                  pl.BlockSpec((None, k, h2), lambda i: (li, 0, 0)),
                  pl.BlockSpec((None, 1, N_V_HEADS), lambda i: (li, 0, 0)),
                  pl.BlockSpec((None, 1, N_V_HEADS), lambda i: (li, 0, 0))],
        out_specs=pl.BlockSpec((tm, 3 * N_V_HEADS), lambda i: (i, 0)),
        out_shape=jax.ShapeDtypeStruct((m, 3 * N_V_HEADS), F32),
        compiler_params=_params("parallel"),
        name="gates",
    )(x, w_ab, a_log, dt_bias)


def _qkv_prompt_kernel(x_ref, w_ref, cw_ref, act_ref, st_ref, xs_ref, *, tm, tiles_per_seq):
    i = pl.program_id(1)

    @pl.when(i % tiles_per_seq == 0)
    def _():
        xs_ref[0:SUBLANES, :] = jnp.zeros((SUBLANES, xs_ref.shape[1]), F32)

    cw = cw_ref[...]
    starts = list(range(0, tm, CONV_SUB_ROWS))
    x_next = _dot(x_ref[0:CONV_SUB_ROWS, :], w_ref[...])
    for r0 in starts:
        r1 = r0 + CONV_SUB_ROWS
        x = x_next
        if r1 < tm:
            x_next = _dot(x_ref[r1:r1 + CONV_SUB_ROWS, :], w_ref[...])
        xs_ref[SUBLANES + r0:SUBLANES + r1, :] = x
        y = cw[3:4] * x
        for j in range(QKV_CONV - 1):
            lag = QKV_CONV - 1 - j
            y = y + cw[j:j + 1] * xs_ref[SUBLANES + r0 - lag:SUBLANES + r1 - lag, :]
        act_ref[r0:r1, :] = _silu(y)
    st_ref[...] = xs_ref[tm + SUBLANES - (QKV_CONV - 1):tm + SUBLANES, :]
    xs_ref[0:SUBLANES, :] = xs_ref[tm:tm + SUBLANES, :]


def _qkv_prompt_call(x, w, conv_w, li, seq, tm, tn):
    m, k = x.shape
    tps = seq // tm
    return pl.pallas_call(
        functools.partial(_qkv_prompt_kernel, tm=tm, tiles_per_seq=tps),
        grid=(QKV_DIM // tn, m // tm),
        in_specs=[pl.BlockSpec((tm, k), lambda n, i: (i, 0)),
                  pl.BlockSpec((None, k, tn), lambda n, i: (li, 0, n)),
                  pl.BlockSpec((None, QKV_CONV, tn), lambda n, i: (li, 0, n))],
        out_specs=[pl.BlockSpec((tm, tn), lambda n, i: (i, n)),
                   pl.BlockSpec((None, QKV_CONV - 1, tn), lambda n, i: (i // tps, 0, n))],
        out_shape=[jax.ShapeDtypeStruct((m, QKV_DIM), F32),
                   jax.ShapeDtypeStruct((m // seq, QKV_CONV - 1, QKV_DIM), F32)],
        scratch_shapes=[pltpu.VMEM((tm + SUBLANES, tn), F32)],
        compiler_params=_params("arbitrary", "arbitrary"),
        name="qkv_conv_prompt",
    )(x, w, conv_w)


def _qkv_sample_kernel(x_ref, w_ref, cw_ref, s_ref, act_ref, st_ref):
    x = _dot(x_ref[...], w_ref[...])
    cw = cw_ref[...]
    y = cw[3:4] * x
    for j in range(QKV_CONV - 1):
        y = y + cw[j:j + 1] * s_ref[j]
    act_ref[...] = _silu(y)
    st_ref[0] = s_ref[1]
    st_ref[1] = s_ref[2]
    st_ref[2] = x


def _qkv_sample_call(x, w, conv_w, state_t, li, tn):
    m, k = x.shape
    return pl.pallas_call(
        _qkv_sample_kernel,
        grid=(QKV_DIM // tn,),
        in_specs=[pl.BlockSpec((m, k), lambda n: (0, 0)),
                  pl.BlockSpec((None, k, tn), lambda n: (li, 0, n)),
                  pl.BlockSpec((None, QKV_CONV, tn), lambda n: (li, 0, n)),
                  pl.BlockSpec((None, QKV_CONV - 1, m, tn), lambda n: (li, 0, 0, n))],
        out_specs=[pl.BlockSpec((m, tn), lambda n: (0, n)),
                   pl.BlockSpec((QKV_CONV - 1, m, tn), lambda n: (0, 0, n))],
        out_shape=[jax.ShapeDtypeStruct((m, QKV_DIM), F32),
                   jax.ShapeDtypeStruct((QKV_CONV - 1, m, QKV_DIM), F32)],
        compiler_params=_params("parallel"),
        name="qkv_conv_sample",
    )(x, w, conv_w, state_t)


def _sconv_prompt_kernel(x_ref, wb_ref, wc_ref, wh_ref, cw_ref, y_ref, st_ref, us_ref, *,
                         tm, tiles_per_seq):
    i = pl.program_id(1)

    @pl.when(i % tiles_per_seq == 0)
    def _():
        us_ref[0:SUBLANES, :] = jnp.zeros((SUBLANES, us_ref.shape[1]), F32)

    cw = cw_ref[...]
    x = x_ref[...]
    u = _dot(x, wc_ref[...]) * _dot(x, wh_ref[...])
    us_ref[SUBLANES:tm + SUBLANES, :] = u
    conv = cw[2:3] * u
    for j in range(SC_CONV - 1):
        lag = SC_CONV - 1 - j
        conv = conv + cw[j:j + 1] * us_ref[SUBLANES - lag:tm + SUBLANES - lag, :]
    y_ref[...] = (_dot(x, wb_ref[...]) * conv).astype(y_ref.dtype)
    st_ref[...] = us_ref[tm + SUBLANES - (SC_CONV - 1):tm + SUBLANES, :]
    us_ref[0:SUBLANES, :] = us_ref[tm:tm + SUBLANES, :]


def _sconv_prompt_call(x, w_rest, conv_w, li, seq, tm, tn):
    m, k = x.shape
    tps = seq // tm
    cb, cc, ch = REST_SB // tn, REST_SC // tn, REST_SH // tn
    return pl.pallas_call(
        functools.partial(_sconv_prompt_kernel, tm=tm, tiles_per_seq=tps),
        grid=(SC_DIM // tn, m // tm),
        in_specs=[pl.BlockSpec((tm, k), lambda n, i: (i, 0)),
                  pl.BlockSpec((None, k, tn), lambda n, i: (li, 0, cb + n)),
                  pl.BlockSpec((None, k, tn), lambda n, i: (li, 0, cc + n)),
                  pl.BlockSpec((None, k, tn), lambda n, i: (li, 0, ch + n)),
                  pl.BlockSpec((None, SC_CONV, tn), lambda n, i: (li, 0, n))],
        out_specs=[pl.BlockSpec((tm, tn), lambda n, i: (i, n)),
                   pl.BlockSpec((None, SC_CONV - 1, tn), lambda n, i: (i // tps, 0, n))],
        out_shape=[jax.ShapeDtypeStruct((m, SC_DIM), BF16),
                   jax.ShapeDtypeStruct((m // seq, SC_CONV - 1, SC_DIM), F32)],
        scratch_shapes=[pltpu.VMEM((tm + SUBLANES, tn), F32)],
        compiler_params=_params("arbitrary", "arbitrary"),
        name="sconv_prompt",
    )(x, w_rest, w_rest, w_rest, conv_w)


def _sconv_sample_kernel(x_ref, wb_ref, wc_ref, wh_ref, cw_ref, s_ref, y_ref, st_ref):
    x = x_ref[...]
    u = _dot(x, wc_ref[...]) * _dot(x, wh_ref[...])
    cw = cw_ref[...]
    conv = cw[2:3] * u
    for j in range(SC_CONV - 1):
        conv = conv + cw[j:j + 1] * s_ref[j]
    y_ref[...] = (_dot(x, wb_ref[...]) * conv).astype(y_ref.dtype)
    st_ref[0] = s_ref[1]
    st_ref[1] = u


def _sconv_sample_call(x, w_rest, conv_w, state_t, li, tn):
    m, k = x.shape
    cb, cc, ch = REST_SB // tn, REST_SC // tn, REST_SH // tn
    return pl.pallas_call(
        _sconv_sample_kernel,
        grid=(SC_DIM // tn,),
        in_specs=[pl.BlockSpec((m, k), lambda n: (0, 0)),
                  pl.BlockSpec((None, k, tn), lambda n: (li, 0, cb + n)),
                  pl.BlockSpec((None, k, tn), lambda n: (li, 0, cc + n)),
                  pl.BlockSpec((None, k, tn), lambda n: (li, 0, ch + n)),
                  pl.BlockSpec((None, SC_CONV, tn), lambda n: (li, 0, n)),
                  pl.BlockSpec((None, SC_CONV - 1, m, tn), lambda n: (li, 0, 0, n))],
        out_specs=[pl.BlockSpec((m, tn), lambda n: (0, n)),
                   pl.BlockSpec((SC_CONV - 1, m, tn), lambda n: (0, 0, n))],
        out_shape=[jax.ShapeDtypeStruct((m, SC_DIM), BF16),
                   jax.ShapeDtypeStruct((SC_CONV - 1, m, SC_DIM), F32)],
        compiler_params=_params("parallel"),
        name="sconv_sample",
    )(x, w_rest, w_rest, w_rest, conv_w, state_t)


def _gdn_prompt_kernel(q_ref, k_ref, v_ref, z_ref, gp_ref, gn_ref, y_ref, s_ref, *, hb):
    c = CHUNK
    c2 = 2 * c
    npairs = hb // 2

    @pl.when(pl.program_id(2) == 0)
    def _():
        s_ref[...] = jnp.zeros(s_ref.shape, F32)

    ri = lax.broadcasted_iota(jnp.int32, (c2, c2), 0)
    ci = lax.broadcasted_iota(jnp.int32, (c2, c2), 1)
    r_in_rblk = (ri & (c - 1)).astype(jnp.uint32)
    c_in_rblk = (ci - (ri - (ri & (c - 1)))).astype(jnp.uint32)
    r_in_cblk = (ri - (ci - (ci & (c - 1)))).astype(jnp.uint32)
    c_in_cblk = (ci & (c - 1)).astype(jnp.uint32)
    lower = c_in_rblk <= r_in_rblk
    strict = c_in_rblk < r_in_rblk
    same = c_in_rblk < jnp.uint32(c)
    upper = r_in_cblk <= c_in_cblk

    gp = gp_ref[...]
    g = gp[:, :npairs]
    beta = gp[:, npairs:]
    g_col = _dot_hi(lower.astype(F32), g)
    g_row = _dot_tn(g, upper.astype(F32), precision=HI)
    g_last = _dot_hi(same.astype(F32), g)
    e_g = jnp.exp(g_col)
    e_rest = jnp.exp(g_last - g_col)
    e_last = jnp.exp(g_last)
    beta_e_g = beta * e_g
    gn = gn_ref[...]

    pairs = range(npairs)
    heads = [(p, r) for p in pairs for r in range(2)]
    k_st, q_st, k_b, kq = [], [], [], []
    for p in pairs:
        q = q_ref[:, p * HEAD_K:(p + 1) * HEAD_K]
        k = k_ref[:, p * HEAD_K:(p + 1) * HEAD_K]
        qn = q * (lax.rsqrt(jnp.sum(q * q, axis=-1, keepdims=True) + EPS) * (HEAD_K ** -0.5))
        kn = k * lax.rsqrt(jnp.sum(k * k, axis=-1, keepdims=True) + EPS)
        k_st.append(jnp.concatenate([kn, kn], axis=0))
        q_st.append(jnp.concatenate([qn, qn], axis=0))
        k_b.append(k_st[p].astype(BF16))
        kq.append(_dot_nt(jnp.concatenate([k_b[p], q_st[p].astype(BF16)], axis=0), k_b[p]))
    decay = [jnp.exp(jnp.where(lower, g_col[:, p:p + 1] - g_row[p:p + 1, :], -jnp.inf)) for p in pairs]
    a = [jnp.where(strict, beta[:, p:p + 1] * kq[p][:c2] * decay[p], 0.0) for p in pairs]
    qkd_b = [(kq[p][c2:] * decay[p]).astype(BF16) for p in pairs]
    a_b = [x.astype(BF16) for x in a]
    corr = [-x for x in a]
    pw = [_dot(x, x) for x in a_b]
    span = 2
    while span < c:
        pw_b = [x.astype(BF16) for x in pw]
        corr = [corr[p] + pw[p] + _dot(corr[p].astype(BF16), pw_b[p]) for p in pairs]
        span *= 2
        if span < c:
            pw = [_dot(x, x) for x in pw_b]
    sol = []
    for p in pairs:
        v_st = jnp.concatenate([v_ref[:, (2 * p + r) * HEAD_V:(2 * p + r + 1) * HEAD_V]
                                for r in range(2)], axis=0)
        rhs = jnp.concatenate([v_st * beta[:, p:p + 1], k_st[p] * beta_e_g[:, p:p + 1]], axis=1)
        sol.append(rhs + _dot(corr[p].astype(BF16), rhs.astype(BF16)))
    qe_st = [q_st[p] * e_g[:, p:p + 1] for p in pairs]
    kd_b = [(k_st[p] * e_rest[:, p:p + 1]).astype(BF16) for p in pairs]
    s_old = [s_ref[2 * p + r] for p, r in heads]
    ws_qs = [_dot(jnp.concatenate([sol[p][r * c:(r + 1) * c, HEAD_V:], qe_st[p][r * c:(r + 1) * c]],
                                  axis=0).astype(BF16), s_old[2 * p + r].astype(BF16))
             for p, r in heads]
    u_new = [sol[p][r * c:(r + 1) * c, :HEAD_V] - ws_qs[2 * p + r][:c] for p, r in heads]
    u_new_b = [x.astype(BF16) for x in u_new]
    for p, r in heads:
        h = 2 * p + r
        s_ref[h] = (s_old[h] * e_last[r * c:r * c + 1, p:p + 1]
                    + _dot_tn(kd_b[p][r * c:(r + 1) * c], u_new_b[h]))
    for p in pairs:
        o_st = (jnp.concatenate([ws_qs[2 * p][c:], ws_qs[2 * p + 1][c:]], axis=0)
                + _dot(qkd_b[p], jnp.concatenate([u_new_b[2 * p], u_new_b[2 * p + 1]], axis=0)))
        y_st = _rms(o_st, gn)
        for r in range(2):
            cols = slice((2 * p + r) * HEAD_V, (2 * p + r + 1) * HEAD_V)
            y_ref[:, cols] = (y_st[r * c:(r + 1) * c] * z_ref[:, cols]).astype(y_ref.dtype)


def _gate_pair_layout(x, hb):
    rows, heads = x.shape
    x = x.reshape(rows // CHUNK, CHUNK, heads // hb, hb // 2, 2)
    return x.transpose(2, 0, 4, 1, 3).reshape(heads // hb, rows // CHUNK, 2 * CHUNK, hb // 2)


def _gdn_prompt_call(qkv, z_act, gate_pairs, gdn_norm, li, seq, hb):
    m = qkv.shape[0]
    nb, nc = m // seq, seq // CHUNK
    qw, vw = HEAD_K * hb // 2, HEAD_V * hb
    kb, vb = KEY_DIM // qw, 2 * KEY_DIM // vw
    return pl.pallas_call(
        functools.partial(_gdn_prompt_kernel, hb=hb),
        grid=(nb, N_V_HEADS // hb, nc),
        in_specs=[pl.BlockSpec((CHUNK, qw), lambda b, h, c: (b * nc + c, h)),
                  pl.BlockSpec((CHUNK, qw), lambda b, h, c: (b * nc + c, kb + h)),
                  pl.BlockSpec((CHUNK, vw), lambda b, h, c: (b * nc + c, vb + h)),
                  pl.BlockSpec((CHUNK, vw), lambda b, h, c: (b * nc + c, h)),
                  pl.BlockSpec((None, None, 2 * CHUNK, hb), lambda b, h, c: (h, b * nc + c, 0, 0)),
                  pl.BlockSpec((None, 1, HEAD_V), lambda b, h, c: (li, 0, 0))],
        out_specs=[pl.BlockSpec((CHUNK, vw), lambda b, h, c: (b * nc + c, h)),
                   pl.BlockSpec((None, hb, HEAD_K, HEAD_V), lambda b, h, c: (b, h, 0, 0))],
        out_shape=[jax.ShapeDtypeStruct((m, VAL_DIM), BF16),
                   jax.ShapeDtypeStruct((nb, N_V_HEADS, HEAD_K, HEAD_V), F32)],
        compiler_params=_params("parallel", "parallel", "arbitrary"),
        name="gdn_prompt",
    )(qkv, qkv, qkv, z_act, gate_pairs, gdn_norm)


def _lanes_to_sublanes(row):
    n = row.shape[1]
    ri = lax.broadcasted_iota(jnp.int32, (n, n), 0)
    ci = lax.broadcasted_iota(jnp.int32, (n, n), 1)
    diag = jnp.where(ri == ci, jnp.broadcast_to(row, (n, n)), 0.0)
    return jnp.broadcast_to(jnp.sum(diag, axis=1, keepdims=True), (n, n))


def _gdn_sample_kernel(gate_ref, q_ref, k_ref, v_ref, z_ref, gn_ref, s_ref, stacked_ref, y_ref, so_ref,
                       *, hb, bt):
    del stacked_ref
    hblk = pl.program_id(0)
    tile = pl.program_id(1)
    gn = gn_ref[...]
    samples = range(bt)
    q_col, k_col = {}, {}
    for j in range(hb // 2):
        q = q_ref[:, j * HEAD_K:(j + 1) * HEAD_K]
        k = k_ref[:, j * HEAD_K:(j + 1) * HEAD_K]
        qn = q * (lax.rsqrt(jnp.sum(q * q, axis=-1, keepdims=True) + EPS) * (HEAD_K ** -0.5))
        kn = k * lax.rsqrt(jnp.sum(k * k, axis=-1, keepdims=True) + EPS)
        for b in samples:
            q_col[b, j] = _lanes_to_sublanes(qn[b:b + 1])
            k_col[b, j] = _lanes_to_sublanes(kn[b:b + 1])
    for h in range(hb):
        cols = slice(h * HEAD_V, (h + 1) * HEAD_V)
        j = h // 2
        o_rows = []
        for b in samples:
            base = (tile * bt + b) * (2 * N_V_HEADS) + hblk * hb + h
            beta = gate_ref[base]
            e_g = gate_ref[base + N_V_HEADS]
            s = s_ref[b, h]
            s_k = jnp.sum(s * k_col[b, j], axis=0, keepdims=True)
            delta = beta * (v_ref[b:b + 1, cols] - e_g * s_k)
            s_new = e_g * s + k_col[b, j] * delta
            so_ref[b, h] = s_new
            o_rows.append(jnp.sum(s_new * q_col[b, j], axis=0, keepdims=True))
        o = jnp.concatenate(o_rows, axis=0)
        y_ref[:, cols] = _rms(o, gn) * z_ref[:, cols]


def _gdn_sample_call(gates_flat, qkv, z_act, gdn_norm, state, new_states, li, hb, bt):
    m = qkv.shape[0]
    qw, vw = HEAD_K * hb // 2, HEAD_V * hb
    kb, vb = KEY_DIM // qw, 2 * KEY_DIM // vw
    state_spec = pl.BlockSpec((None, bt, hb, HEAD_K, HEAD_V), lambda h, t: (li, t, h, 0, 0))
    in_specs = [pl.BlockSpec(memory_space=pltpu.SMEM),
                pl.BlockSpec((bt, qw), lambda h, t: (t, h)),
                pl.BlockSpec((bt, qw), lambda h, t: (t, kb + h)),
                pl.BlockSpec((bt, vw), lambda h, t: (t, vb + h)),
                pl.BlockSpec((bt, vw), lambda h, t: (t, h)),
                pl.BlockSpec((None, 1, HEAD_V), lambda h, t: (li, 0, 0)),
                state_spec,
                pl.BlockSpec(memory_space=pl.ANY)]
    return pl.pallas_call(
        functools.partial(_gdn_sample_kernel, hb=hb, bt=bt),
        grid=(N_V_HEADS // hb, m // bt),
        in_specs=in_specs,
        out_specs=[pl.BlockSpec((bt, vw), lambda h, t: (t, h)), state_spec],
        out_shape=[jax.ShapeDtypeStruct((m, VAL_DIM), F32),
                   jax.ShapeDtypeStruct(state.shape, F32)],
        input_output_aliases={len(in_specs) - 1: 1},
        compiler_params=_params("parallel", "parallel"),
        name="gdn_sample",
    )(gates_flat, qkv, qkv, qkv, z_act, gdn_norm, state, new_states)


def _merge_kernel(ya_ref, yb_ref, ga_ref, gb_ref, wa_ref, wb_ref, o_ref):
    a = _dot(ya_ref[...].astype(BF16), wa_ref[...])
    b = _dot(yb_ref[...], wb_ref[...])
    o_ref[...] = (ga_ref[...] * a + gb_ref[...] * b).astype(o_ref.dtype)


def _merge_call(ya, yb, gates, w_gdn_out, w_sconv_out, li, tm, tn):
    m = ya.shape[0]
    gbo = D_MODEL // tn
    return pl.pallas_call(
        _merge_kernel,
        grid=(D_MODEL // tn, m // tm),
        in_specs=[pl.BlockSpec((tm, VAL_DIM), lambda n, i: (i, 0)),
                  pl.BlockSpec((tm, SC_DIM), lambda n, i: (i, 0)),
                  pl.BlockSpec((tm, tn), lambda n, i: (i, n)),
                  pl.BlockSpec((tm, tn), lambda n, i: (i, gbo + n)),
                  pl.BlockSpec((None, VAL_DIM, tn), lambda n, i: (li, 0, n)),
                  pl.BlockSpec((None, SC_DIM, tn), lambda n, i: (li, 0, n))],
        out_specs=pl.BlockSpec((tm, tn), lambda n, i: (i, n)),
        out_shape=jax.ShapeDtypeStruct((m, D_MODEL), BF16),
        compiler_params=_params("parallel", "parallel"),
        name="merge",
    )(ya, yb, gates, gates, w_gdn_out, w_sconv_out)


def _oproj_kernel(x_ref, h_ref, w_ref, g_ref, h_out, n_out):
    h = h_ref[...] + _dot(x_ref[...], w_ref[...])
    h_out[...] = h
    n_out[...] = _rms(h, g_ref[...]).astype(n_out.dtype)


def _oproj_call(x, h, w_o, ffn_norm, li, tm):
    m, d = h.shape
    return pl.pallas_call(
        _oproj_kernel,
        grid=(m // tm,),
        in_specs=[pl.BlockSpec((tm, d), lambda i: (i, 0)),
                  pl.BlockSpec((tm, d), lambda i: (i, 0)),
                  pl.BlockSpec((None, d, d), lambda i: (li, 0, 0)),
                  pl.BlockSpec((None, 1, d), lambda i: (li, 0, 0))],
        out_specs=[pl.BlockSpec((tm, d), lambda i: (i, 0)),
                   pl.BlockSpec((tm, d), lambda i: (i, 0))],
        out_shape=[jax.ShapeDtypeStruct((m, d), F32),
                   jax.ShapeDtypeStruct((m, d), BF16)],
        compiler_params=_params("parallel"),
        name="oproj",
    )(x, h, w_o, ffn_norm)


def _ffn_up_kernel(x_ref, wg_ref, wu_ref, o_ref):
    x = x_ref[...]
    o_ref[...] = (_silu(_dot(x, wg_ref[...])) * _dot(x, wu_ref[...])).astype(o_ref.dtype)


def _ffn_up_call(x, w_gate, w_up, li, tm, tn):
    m, k = x.shape
    return pl.pallas_call(
        _ffn_up_kernel,
        grid=(D_FF // tn, m // tm),
        in_specs=[pl.BlockSpec((tm, k), lambda n, i: (i, 0)),
                  pl.BlockSpec((None, k, tn), lambda n, i: (li, 0, n)),
                  pl.BlockSpec((None, k, tn), lambda n, i: (li, 0, n))],
        out_specs=pl.BlockSpec((tm, tn), lambda n, i: (i, n)),
        out_shape=jax.ShapeDtypeStruct((m, D_FF), BF16),
        compiler_params=_params("parallel", "parallel"),
        name="ffn_up",
    )(x, w_gate, w_up)


def _ffn_down_kernel(x_ref, h_ref, w_ref, o_ref):
    o_ref[...] = h_ref[...] + _dot(x_ref[...], w_ref[...])


def _ffn_down_call(x, h, w_down, li, tm, tn):
    m, k = x.shape
    return pl.pallas_call(
        _ffn_down_kernel,
        grid=(D_MODEL // tn, m // tm),
        in_specs=[pl.BlockSpec((tm, k), lambda n, i: (i, 0)),
                  pl.BlockSpec((tm, tn), lambda n, i: (i, n)),
                  pl.BlockSpec((None, k, tn), lambda n, i: (li, 0, n))],
        out_specs=pl.BlockSpec((tm, tn), lambda n, i: (i, n)),
        out_shape=jax.ShapeDtypeStruct((m, D_MODEL), F32),
        compiler_params=_params("parallel", "parallel"),
        name="ffn_down",
    )(x, h, w_down)


def _ple_kernel(h_ref, p_ref, wg_ref, wp_ref, g_ref, gnext_ref, h_out, n_out):
    h = h_ref[...]
    hn = _rms(h, g_ref[...]).astype(BF16)
    gate = jax.nn.sigmoid(_dot(hn, wg_ref[...]))
    h = h + gate * _dot(p_ref[...].astype(BF16), wp_ref[...])
    h_out[...] = h
    n_out[...] = _rms(h, gnext_ref[...]).astype(n_out.dtype)


def _ple_call(h, p, w_gate, w_proj, ple_norm, next_norm, li, next_li, next_dtype, tm):
    m, d = h.shape
    return pl.pallas_call(
        _ple_kernel,
        grid=(m // tm,),
        in_specs=[pl.BlockSpec((tm, d), lambda i: (i, 0)),
                  pl.BlockSpec((None, tm, PLE_DIM), lambda i: (li, i, 0)),
                  pl.BlockSpec((None, d, d), lambda i: (li, 0, 0)),
                  pl.BlockSpec((None, PLE_DIM, d), lambda i: (li, 0, 0)),
                  pl.BlockSpec((None, 1, d), lambda i: (li, 0, 0)),
                  pl.BlockSpec((None, 1, d), lambda i: (next_li, 0, 0))],
        out_specs=[pl.BlockSpec((tm, d), lambda i: (i, 0)),
                   pl.BlockSpec((tm, d), lambda i: (i, 0))],
        out_shape=[jax.ShapeDtypeStruct((m, d), F32),
                   jax.ShapeDtypeStruct((m, d), next_dtype)],
        compiler_params=_params("parallel"),
        name="ple",
    )(h, p, w_gate, w_proj, ple_norm, next_norm)


def _layer(li, h, xn, p, w, seq, states):
    m = h.shape[0]
    prompt = seq > 1
    tm = 512 if prompt else m
    tm_row = 256 if prompt else m
    hb = GDN_PROMPT_HEADS_PER_STEP if prompt else GDN_SAMPLE_HEADS_PER_STEP

    gates3 = _gates_call(xn, w["ab"], w["a_log"], w["dt_bias"], li, tm)
    z_act = _proj_act_call(xn, w["main"], li, OFF_Z, VAL_DIM, _silu, tm, 1024, "proj_z")
    mix_gates = _proj_act_call(xn, w["rest"], li, REST_G, 2 * D_MODEL, jax.nn.sigmoid, tm, 1024,
                               "proj_mix_gates")

    if prompt:
        qkv, new_qkv = _qkv_prompt_call(xn, w["main"], w["qkv_conv"], li, seq, tm, 512)
        yb, new_sc = _sconv_prompt_call(xn, w["rest"], w["sconv"], li, seq, tm, 512)
        gate_pairs = jnp.concatenate(
            [_gate_pair_layout(gates3[:, :N_V_HEADS], hb),
             _gate_pair_layout(gates3[:, N_V_HEADS:2 * N_V_HEADS], hb)], axis=-1)
        ya, new_gdn = _gdn_prompt_call(qkv, z_act, gate_pairs, w["gdn_norm"], li, seq, hb)
    else:
        s_gdn, s_qkv_t, s_sc_t, new_gdn_stacked = states
        qkv, new_qkv_t = _qkv_sample_call(xn, w["main"], w["qkv_conv"], s_qkv_t, li, 1024)
        yb, new_sc_t = _sconv_sample_call(xn, w["rest"], w["sconv"], s_sc_t, li, 512)
        new_qkv = new_qkv_t.transpose(1, 0, 2)
        new_sc = new_sc_t.transpose(1, 0, 2)
        gates_flat = gates3[:, N_V_HEADS:].reshape(-1)
        ya, new_gdn = _gdn_sample_call(gates_flat, qkv, z_act, w["gdn_norm"], s_gdn, new_gdn_stacked,
                                       li, hb, SAMPLE_TILE)

    mixed = _merge_call(ya, yb, mix_gates, w["gdn_out"], w["sconv_out"], li, tm_row, 1024)
    h2, hn2 = _oproj_call(mixed, h, w["o"], w["ffn_norm"], li, tm_row)
    act = _ffn_up_call(hn2, w["ffn_gate"], w["ffn_up"], li, tm, 512)
    h3 = _ffn_down_call(act, h2, w["ffn_down"], li, tm_row, 1024)
    last = li == DEPTH - 1
    next_norm = w["final_norm"] if last else w["attn_norm"]
    h4, nxt = _ple_call(h3, p, w["ple_gate"], w["ple_proj"], w["ple_norm"], next_norm, li,
                        0 if last else li + 1, F32 if last else BF16, tm_row)
    return h4, nxt, new_gdn, new_qkv, new_sc


def kernel(x_prompt, x_sample, state_gdn, state_qkv_conv, state_short_conv, p_prompt, p_sample, attn_norm, w_in, qkv_conv_w, a_log, dt_bias, gdn_norm, w_gdn_out, sconv_w, w_sconv_out, w_o, ffn_norm, w_ffn_gate, w_ffn_up, w_ffn_down, ple_norm, w_ple_gate, w_ple_proj, final_norm):
    nb, seq, d = x_prompt.shape
    ns = x_sample.shape[0]
    w = {
        "main": _cast_columns_call(w_in, 0, OFF_AB, 512, 1024),
        "ab": w_in[:, :, OFF_AB:OFF_REST].astype(BF16),
        "rest": _cast_columns_call(w_in, OFF_REST, w_in.shape[2] - OFF_REST, 512, 512),
        "qkv_conv": qkv_conv_w,
        "sconv": sconv_w,
        "a_log": a_log[:, None, :],
        "dt_bias": dt_bias[:, None, :],
        "gdn_norm": gdn_norm[:, None, :],
        "gdn_out": w_gdn_out.astype(BF16),
        "sconv_out": w_sconv_out.astype(BF16),
        "o": w_o.astype(BF16),
        "ffn_gate": w_ffn_gate.astype(BF16),
        "ffn_up": w_ffn_up.astype(BF16),
        "ffn_down": w_ffn_down.astype(BF16),
        "ple_gate": w_ple_gate.astype(BF16),
        "ple_proj": w_ple_proj.astype(BF16),
        "attn_norm": attn_norm[:, None, :],
        "ffn_norm": ffn_norm[:, None, :],
        "ple_norm": ple_norm[:, None, :],
        "final_norm": final_norm[None, None, :],
    }
    hp = x_prompt.reshape(nb * seq, d)
    hs = x_sample.reshape(ns, d)
    pp = p_prompt.reshape(DEPTH, nb * seq, PLE_DIM)
    ps = p_sample.reshape(DEPTH, ns, PLE_DIM)
    s_qkv_t = state_qkv_conv.transpose(0, 2, 1, 3)
    s_sc_t = state_short_conv.transpose(0, 2, 1, 3)

    xp = _rmsnorm_call(hp, w["attn_norm"], 0, 256)
    xs = _rmsnorm_call(hs, w["attn_norm"], 0, ns)
    outs_p, outs_s = [], []
    new_gdn_s = jnp.zeros(state_gdn.shape, F32)
    for li in range(DEPTH):
        hp, xp, *st = _layer(li, hp, xp, pp, w, seq, None)
        outs_p.append(st)
        hs, xs, new_gdn_s, *st = _layer(li, hs, xs, ps, w, 1, (state_gdn, s_qkv_t, s_sc_t, new_gdn_s))
        outs_s.append(st)
    y_prompt = xp.reshape(nb, seq, d)
    y_sample = xs.reshape(ns, 1, d)
    stack = lambda outs, j: jnp.stack([o[j] for o in outs])
    return (y_prompt, y_sample, stack(outs_p, 0), stack(outs_p, 1), stack(outs_p, 2),
            new_gdn_s, stack(outs_s, 0), stack(outs_s, 1))
```

```python
import functools

import jax
import jax.numpy as jnp
from jax import lax
from jax.experimental import pallas as pl
from jax.experimental.pallas import tpu as pltpu

F32 = jnp.float32
BF16 = jnp.bfloat16
HI = lax.Precision.HIGHEST

D_MODEL = 2048
DEPTH = 4
N_QK_HEADS = 16
N_V_HEADS = 32
HEAD_K = 128
HEAD_V = 128
KEY_DIM = N_QK_HEADS * HEAD_K
VAL_DIM = N_V_HEADS * HEAD_V
QKV_DIM = 2 * KEY_DIM + VAL_DIM
QKV_CONV = 4
SC_CONV = 3
SC_DIM = D_MODEL
CHUNK = 64
D_FF = 5632
PLE_DIM = 256
EPS = 1e-6

OFF_Z = QKV_DIM
OFF_AB = OFF_Z + VAL_DIM
OFF_REST = OFF_AB + 2 * N_V_HEADS
REST_SB, REST_SC, REST_SH, REST_G = 0, SC_DIM, 2 * SC_DIM, 3 * SC_DIM

VMEM_LIMIT_BYTES = 56 * 1024 * 1024
SUBLANES = 8
LANES = 128
CONV_SUB_COLS = 256

GDN_PROMPT_HEADS_PER_STEP = 16
GDN_SAMPLE_HEADS_PER_STEP = 4
SAMPLE_TILE = 8


def _params(*sem):
    return pltpu.CompilerParams(dimension_semantics=sem, vmem_limit_bytes=VMEM_LIMIT_BYTES)


def _dot(a, b):
    return jnp.dot(a, b, preferred_element_type=F32)


def _dot_nt(a, b):
    return lax.dot_general(a, b, (((1,), (1,)), ((), ())), preferred_element_type=F32)


def _dot_tn(a, b, precision=None):
    return lax.dot_general(a, b, (((0,), (0,)), ((), ())), precision=precision,
                           preferred_element_type=F32)


def _dot_hi(a, b):
    return jnp.dot(a, b, precision=HI, preferred_element_type=F32)


def _rms(x, g):
    return x * lax.rsqrt(jnp.mean(x * x, axis=-1, keepdims=True) + EPS) * g


def _silu(x):
    return x * jax.nn.sigmoid(x)


def _rmsnorm_kernel(x_ref, g_ref, o_ref):
    o_ref[...] = _rms(x_ref[...], g_ref[...]).astype(o_ref.dtype)


def _rmsnorm_call(x, gain, li, tm):
    m, d = x.shape
    return pl.pallas_call(
        _rmsnorm_kernel,
        grid=(m // tm,),
        in_specs=[pl.BlockSpec((tm, d), lambda i: (i, 0)),
                  pl.BlockSpec((None, 1, d), lambda i: (li, 0, 0))],
        out_specs=pl.BlockSpec((tm, d), lambda i: (i, 0)),
        out_shape=jax.ShapeDtypeStruct((m, d), BF16),
        compiler_params=_params("parallel"),
        name="rmsnorm",
    )(x, gain)


def _cast_t_kernel(a_ref, o_ref):
    o_ref[...] = a_ref[...].T.astype(o_ref.dtype)


def _cast_t_shift_kernel(a_ref, b_ref, o_ref, *, shift):
    o_ref[...] = jnp.concatenate([a_ref[shift:, :], b_ref[...]], axis=0).T.astype(o_ref.dtype)


def _cast_columns_call(w_t, col0, width, tk, tn):
    depth, _, k = w_t.shape
    shift = col0 % tn
    base = col0 // tn
    out_spec = pl.BlockSpec((None, tk, tn), lambda l, r, c: (l, r, c))
    a_spec = pl.BlockSpec((None, tn, tk), lambda l, r, c: (l, base + c, r))
    common = dict(grid=(depth, k // tk, width // tn), out_specs=out_spec,
                  out_shape=jax.ShapeDtypeStruct((depth, k, width), BF16),
                  compiler_params=_params("parallel", "parallel", "parallel"))
    if shift == 0:
        return pl.pallas_call(_cast_t_kernel, in_specs=[a_spec], name="cast_w", **common)(w_t)
    per_block = tn // shift
    b_spec = pl.BlockSpec((None, shift, tk), lambda l, r, c: (l, (base + c + 1) * per_block, r))
    return pl.pallas_call(functools.partial(_cast_t_shift_kernel, shift=shift),
                          in_specs=[a_spec, b_spec], name="cast_shift_w", **common)(w_t, w_t)


def _proj_act_kernel(x_ref, w_ref, o_ref, *, act):
    o_ref[...] = act(_dot(x_ref[...], w_ref[...])).astype(o_ref.dtype)


def _proj_act_call(x, w, li, col0, n_out, act, tm, tn, name):
    m, k = x.shape
    cb = col0 // tn
    return pl.pallas_call(
        functools.partial(_proj_act_kernel, act=act),
        grid=(n_out // tn, m // tm),
        in_specs=[pl.BlockSpec((tm, k), lambda n, i: (i, 0)),
                  pl.BlockSpec((None, k, tn), lambda n, i: (li, 0, cb + n))],
        out_specs=pl.BlockSpec((tm, tn), lambda n, i: (i, n)),
        out_shape=jax.ShapeDtypeStruct((m, n_out), F32),
        compiler_params=_params("parallel", "parallel"),
        name=name,
    )(x, w)


def _gates_kernel(x_ref, w_ref, alog_ref, dtb_ref, o_ref):
    r = _dot(x_ref[...], w_ref[...])
    a = r[:, :N_V_HEADS] + dtb_ref[...]
    softplus = jnp.maximum(a, 0.0) + jnp.log(1.0 + jnp.exp(-jnp.abs(a)))
    g = -jnp.exp(alog_ref[...]) * softplus
    o_ref[:, 0:N_V_HEADS] = g
    o_ref[:, N_V_HEADS:2 * N_V_HEADS] = jax.nn.sigmoid(r[:, N_V_HEADS:2 * N_V_HEADS])
    o_ref[:, 2 * N_V_HEADS:3 * N_V_HEADS] = jnp.exp(g)


def _gates_call(x, w_ab, a_log, dt_bias, li, tm):
    m, k = x.shape
    return pl.pallas_call(
        _gates_kernel,
        grid=(m // tm,),
        in_specs=[pl.BlockSpec((tm, k), lambda i: (i, 0)),
                  pl.BlockSpec((None, k, LANES), lambda i: (li, 0, 0)),
                  pl.BlockSpec((None, 1, N_V_HEADS), lambda i: (li, 0, 0)),
                  pl.BlockSpec((None, 1, N_V_HEADS), lambda i: (li, 0, 0))],
        out_specs=pl.BlockSpec((tm, 3 * N_V_HEADS), lambda i: (i, 0)),
        out_shape=jax.ShapeDtypeStruct((m, 3 * N_V_HEADS), F32),
        compiler_params=_params("parallel"),
        name="gates",
    )(x, w_ab, a_log, dt_bias)


def _qkv_prompt_kernel(x_ref, w_ref, cw_ref, act_ref, st_ref, xs_ref, *, tm, tiles_per_seq):
    i = pl.program_id(1)

    @pl.when(i % tiles_per_seq == 0)
    def _():
        xs_ref[0:SUBLANES, :] = jnp.zeros((SUBLANES, xs_ref.shape[1]), F32)

    for c0 in range(0, xs_ref.shape[1], CONV_SUB_COLS):
        cols = slice(c0, c0 + CONV_SUB_COLS)
        x = _dot(x_ref[...], w_ref[:, cols])
        xs_ref[SUBLANES:tm + SUBLANES, cols] = x
        cw = cw_ref[:, cols]
        y = cw[3:4] * x
        for j in range(QKV_CONV - 1):
            lag = QKV_CONV - 1 - j
            y = y + cw[j:j + 1] * xs_ref[SUBLANES - lag:tm + SUBLANES - lag, cols]
        act_ref[:, cols] = _silu(y)
    st_ref[...] = xs_ref[tm + SUBLANES - (QKV_CONV - 1):tm + SUBLANES, :]
    xs_ref[0:SUBLANES, :] = xs_ref[tm:tm + SUBLANES, :]


def _qkv_prompt_call(x, w, conv_w, li, seq, tm, tn):
    m, k = x.shape
    tps = seq // tm
    return pl.pallas_call(
        functools.partial(_qkv_prompt_kernel, tm=tm, tiles_per_seq=tps),
        grid=(QKV_DIM // tn, m // tm),
        in_specs=[pl.BlockSpec((tm, k), lambda n, i: (i, 0)),
                  pl.BlockSpec((None, k, tn), lambda n, i: (li, 0, n)),
                  pl.BlockSpec((None, QKV_CONV, tn), lambda n, i: (li, 0, n))],
        out_specs=[pl.BlockSpec((tm, tn), lambda n, i: (i, n)),
                   pl.BlockSpec((None, QKV_CONV - 1, tn), lambda n, i: (i // tps, 0, n))],
        out_shape=[jax.ShapeDtypeStruct((m, QKV_DIM), F32),
                   jax.ShapeDtypeStruct((m // seq, QKV_CONV - 1, QKV_DIM), F32)],
        scratch_shapes=[pltpu.VMEM((tm + SUBLANES, tn), F32)],
        compiler_params=_params("arbitrary", "arbitrary"),
        name="qkv_conv_prompt",
    )(x, w, conv_w)


def _qkv_sample_kernel(x_ref, w_ref, cw_ref, s_ref, act_ref, st_ref):
    x = _dot(x_ref[...], w_ref[...])
    cw = cw_ref[...]
    y = cw[3:4] * x
    for j in range(QKV_CONV - 1):
        y = y + cw[j:j + 1] * s_ref[j]
    act_ref[...] = _silu(y)
    st_ref[0] = s_ref[1]
    st_ref[1] = s_ref[2]
    st_ref[2] = x


def _qkv_sample_call(x, w, conv_w, state_t, li, tn):
    m, k = x.shape
    return pl.pallas_call(
        _qkv_sample_kernel,
        grid=(QKV_DIM // tn,),
        in_specs=[pl.BlockSpec((m, k), lambda n: (0, 0)),
                  pl.BlockSpec((None, k, tn), lambda n: (li, 0, n)),
                  pl.BlockSpec((None, QKV_CONV, tn), lambda n: (li, 0, n)),
                  pl.BlockSpec((None, QKV_CONV - 1, m, tn), lambda n: (li, 0, 0, n))],
        out_specs=[pl.BlockSpec((m, tn), lambda n: (0, n)),
                   pl.BlockSpec((QKV_CONV - 1, m, tn), lambda n: (0, 0, n))],
        out_shape=[jax.ShapeDtypeStruct((m, QKV_DIM), F32),
                   jax.ShapeDtypeStruct((QKV_CONV - 1, m, QKV_DIM), F32)],
        compiler_params=_params("parallel"),
        name="qkv_conv_sample",
    )(x, w, conv_w, state_t)


def _sconv_prompt_kernel(x_ref, wb_ref, wc_ref, wh_ref, cw_ref, y_ref, st_ref, us_ref, *,
                         tm, tiles_per_seq):
    i = pl.program_id(1)

    @pl.when(i % tiles_per_seq == 0)
    def _():
        us_ref[0:SUBLANES, :] = jnp.zeros((SUBLANES, us_ref.shape[1]), F32)

    cw = cw_ref[...]
    x = x_ref[...]
    u = _dot(x, wc_ref[...]) * _dot(x, wh_ref[...])
    us_ref[SUBLANES:tm + SUBLANES, :] = u
    conv = cw[2:3] * u
    for j in range(SC_CONV - 1):
        lag = SC_CONV - 1 - j
        conv = conv + cw[j:j + 1] * us_ref[SUBLANES - lag:tm + SUBLANES - lag, :]
    y_ref[...] = (_dot(x, wb_ref[...]) * conv).astype(y_ref.dtype)
    st_ref[...] = us_ref[tm + SUBLANES - (SC_CONV - 1):tm + SUBLANES, :]
    us_ref[0:SUBLANES, :] = us_ref[tm:tm + SUBLANES, :]


def _sconv_prompt_call(x, w_rest, conv_w, li, seq, tm, tn):
    m, k = x.shape
    tps = seq // tm
    cb, cc, ch = REST_SB // tn, REST_SC // tn, REST_SH // tn
    return pl.pallas_call(
        functools.partial(_sconv_prompt_kernel, tm=tm, tiles_per_seq=tps),
        grid=(SC_DIM // tn, m // tm),
        in_specs=[pl.BlockSpec((tm, k), lambda n, i: (i, 0)),
                  pl.BlockSpec((None, k, tn), lambda n, i: (li, 0, cb + n)),
                  pl.BlockSpec((None, k, tn), lambda n, i: (li, 0, cc + n)),
                  pl.BlockSpec((None, k, tn), lambda n, i: (li, 0, ch + n)),
                  pl.BlockSpec((None, SC_CONV, tn), lambda n, i: (li, 0, n))],
        out_specs=[pl.BlockSpec((tm, tn), lambda n, i: (i, n)),
                   pl.BlockSpec((None, SC_CONV - 1, tn), lambda n, i: (i // tps, 0, n))],
        out_shape=[jax.ShapeDtypeStruct((m, SC_DIM), BF16),
                   jax.ShapeDtypeStruct((m // seq, SC_CONV - 1, SC_DIM), F32)],
        scratch_shapes=[pltpu.VMEM((tm + SUBLANES, tn), F32)],
        compiler_params=_params("arbitrary", "arbitrary"),
        name="sconv_prompt",
    )(x, w_rest, w_rest, w_rest, conv_w)


def _sconv_sample_kernel(x_ref, wb_ref, wc_ref, wh_ref, cw_ref, s_ref, y_ref, st_ref):
    x = x_ref[...]
    u = _dot(x, wc_ref[...]) * _dot(x, wh_ref[...])
    cw = cw_ref[...]
    conv = cw[2:3] * u
    for j in range(SC_CONV - 1):
        conv = conv + cw[j:j + 1] * s_ref[j]
    y_ref[...] = (_dot(x, wb_ref[...]) * conv).astype(y_ref.dtype)
    st_ref[0] = s_ref[1]
    st_ref[1] = u


def _sconv_sample_call(x, w_rest, conv_w, state_t, li, tn):
    m, k = x.shape
    cb, cc, ch = REST_SB // tn, REST_SC // tn, REST_SH // tn
    return pl.pallas_call(
        _sconv_sample_kernel,
        grid=(SC_DIM // tn,),
        in_specs=[pl.BlockSpec((m, k), lambda n: (0, 0)),
                  pl.BlockSpec((None, k, tn), lambda n: (li, 0, cb + n)),
                  pl.BlockSpec((None, k, tn), lambda n: (li, 0, cc + n)),
                  pl.BlockSpec((None, k, tn), lambda n: (li, 0, ch + n)),
                  pl.BlockSpec((None, SC_CONV, tn), lambda n: (li, 0, n)),
                  pl.BlockSpec((None, SC_CONV - 1, m, tn), lambda n: (li, 0, 0, n))],
        out_specs=[pl.BlockSpec((m, tn), lambda n: (0, n)),
                   pl.BlockSpec((SC_CONV - 1, m, tn), lambda n: (0, 0, n))],
        out_shape=[jax.ShapeDtypeStruct((m, SC_DIM), BF16),
                   jax.ShapeDtypeStruct((SC_CONV - 1, m, SC_DIM), F32)],
        compiler_params=_params("parallel"),
        name="sconv_sample",
    )(x, w_rest, w_rest, w_rest, conv_w, state_t)


def _gdn_prompt_kernel(q_ref, k_ref, v_ref, z_ref, gp_ref, gn_ref, y_ref, s_ref, *, hb):
    c = CHUNK
    c2 = 2 * c
    npairs = hb // 2

    @pl.when(pl.program_id(2) == 0)
    def _():
        s_ref[...] = jnp.zeros(s_ref.shape, F32)

    ri = lax.broadcasted_iota(jnp.int32, (c2, c2), 0)
    ci = lax.broadcasted_iota(jnp.int32, (c2, c2), 1)
    r_in_rblk = (ri & (c - 1)).astype(jnp.uint32)
    c_in_rblk = (ci - (ri - (ri & (c - 1)))).astype(jnp.uint32)
    r_in_cblk = (ri - (ci - (ci & (c - 1)))).astype(jnp.uint32)
    c_in_cblk = (ci & (c - 1)).astype(jnp.uint32)
    lower = c_in_rblk <= r_in_rblk
    strict = c_in_rblk < r_in_rblk
    same = c_in_rblk < jnp.uint32(c)
    upper = r_in_cblk <= c_in_cblk

    gp = gp_ref[...]
    g = gp[:, :npairs]
    beta = gp[:, npairs:]
    g_col = _dot_hi(lower.astype(F32), g)
    g_row = _dot_tn(g, upper.astype(F32), precision=HI)
    g_last = _dot_hi(same.astype(F32), g)
    e_g = jnp.exp(g_col)
    e_rest = jnp.exp(g_last - g_col)
    e_last = jnp.exp(g_last)
    beta_e_g = beta * e_g
    gn = gn_ref[...]

    pairs = range(npairs)
    heads = [(p, r) for p in pairs for r in range(2)]
    k_st, q_st, k_b, kq = [], [], [], []
    for p in pairs:
        q = q_ref[:, p * HEAD_K:(p + 1) * HEAD_K]
        k = k_ref[:, p * HEAD_K:(p + 1) * HEAD_K]
        qn = q * (lax.rsqrt(jnp.sum(q * q, axis=-1, keepdims=True) + EPS) * (HEAD_K ** -0.5))
        kn = k * lax.rsqrt(jnp.sum(k * k, axis=-1, keepdims=True) + EPS)
        k_st.append(jnp.concatenate([kn, kn], axis=0))
        q_st.append(jnp.concatenate([qn, qn], axis=0))
        k_b.append(k_st[p].astype(BF16))
        kq.append(_dot_nt(jnp.concatenate([k_b[p], q_st[p].astype(BF16)], axis=0), k_b[p]))
    decay = [jnp.exp(jnp.where(lower, g_col[:, p:p + 1] - g_row[p:p + 1, :], -jnp.inf)) for p in pairs]
    a = [jnp.where(strict, beta[:, p:p + 1] * kq[p][:c2] * decay[p], 0.0) for p in pairs]
    qkd_b = [(kq[p][c2:] * decay[p]).astype(BF16) for p in pairs]
    a_b = [x.astype(BF16) for x in a]
    corr = [-x for x in a]
    pw = [_dot(x, x) for x in a_b]
    span = 2
    while span < c:
        pw_b = [x.astype(BF16) for x in pw]
        corr = [corr[p] + pw[p] + _dot(corr[p].astype(BF16), pw_b[p]) for p in pairs]
        span *= 2
        if span < c:
            pw = [_dot(x, x) for x in pw_b]
    sol = []
    for p in pairs:
        v_st = jnp.concatenate([v_ref[:, (2 * p + r) * HEAD_V:(2 * p + r + 1) * HEAD_V]
                                for r in range(2)], axis=0)
        rhs = jnp.concatenate([v_st * beta[:, p:p + 1], k_st[p] * beta_e_g[:, p:p + 1]], axis=1)
        sol.append(rhs + _dot(corr[p].astype(BF16), rhs.astype(BF16)))
    qe_st = [q_st[p] * e_g[:, p:p + 1] for p in pairs]
    kd_b = [(k_st[p] * e_rest[:, p:p + 1]).astype(BF16) for p in pairs]
    s_old = [s_ref[2 * p + r] for p, r in heads]
    ws_qs = [_dot(jnp.concatenate([sol[p][r * c:(r + 1) * c, HEAD_V:], qe_st[p][r * c:(r + 1) * c]],
                                  axis=0).astype(BF16), s_old[2 * p + r].astype(BF16))
             for p, r in heads]
    u_new = [sol[p][r * c:(r + 1) * c, :HEAD_V] - ws_qs[2 * p + r][:c] for p, r in heads]
    u_new_b = [x.astype(BF16) for x in u_new]
    for p, r in heads:
        h = 2 * p + r
        s_ref[h] = (s_old[h] * e_last[r * c:r * c + 1, p:p + 1]
                    + _dot_tn(kd_b[p][r * c:(r + 1) * c], u_new_b[h]))
    for p in pairs:
        o_st = (jnp.concatenate([ws_qs[2 * p][c:], ws_qs[2 * p + 1][c:]], axis=0)
                + _dot(qkd_b[p], jnp.concatenate([u_new_b[2 * p], u_new_b[2 * p + 1]], axis=0)))
        y_st = _rms(o_st, gn)
        for r in range(2):
            cols = slice((2 * p + r) * HEAD_V, (2 * p + r + 1) * HEAD_V)
            y_ref[:, cols] = (y_st[r * c:(r + 1) * c] * z_ref[:, cols]).astype(y_ref.dtype)


def _gate_pair_layout(x, hb):
    rows, heads = x.shape
    x = x.reshape(rows // CHUNK, CHUNK, heads // hb, hb // 2, 2)
    return x.transpose(2, 0, 4, 1, 3).reshape(heads // hb, rows // CHUNK, 2 * CHUNK, hb // 2)


def _gdn_prompt_call(qkv, z_act, gate_pairs, gdn_norm, li, seq, hb):
    m = qkv.shape[0]
    nb, nc = m // seq, seq // CHUNK
    qw, vw = HEAD_K * hb // 2, HEAD_V * hb
    kb, vb = KEY_DIM // qw, 2 * KEY_DIM // vw
    return pl.pallas_call(
        functools.partial(_gdn_prompt_kernel, hb=hb),
        grid=(nb, N_V_HEADS // hb, nc),
        in_specs=[pl.BlockSpec((CHUNK, qw), lambda b, h, c: (b * nc + c, h)),
                  pl.BlockSpec((CHUNK, qw), lambda b, h, c: (b * nc + c, kb + h)),
                  pl.BlockSpec((CHUNK, vw), lambda b, h, c: (b * nc + c, vb + h)),
                  pl.BlockSpec((CHUNK, vw), lambda b, h, c: (b * nc + c, h)),
                  pl.BlockSpec((None, None, 2 * CHUNK, hb), lambda b, h, c: (h, b * nc + c, 0, 0)),
                  pl.BlockSpec((None, 1, HEAD_V), lambda b, h, c: (li, 0, 0))],
        out_specs=[pl.BlockSpec((CHUNK, vw), lambda b, h, c: (b * nc + c, h)),
                   pl.BlockSpec((None, hb, HEAD_K, HEAD_V), lambda b, h, c: (b, h, 0, 0))],
        out_shape=[jax.ShapeDtypeStruct((m, VAL_DIM), BF16),
                   jax.ShapeDtypeStruct((nb, N_V_HEADS, HEAD_K, HEAD_V), F32)],
        compiler_params=_params("parallel", "parallel", "arbitrary"),
        name="gdn_prompt",
    )(qkv, qkv, qkv, z_act, gate_pairs, gdn_norm)


def _lanes_to_sublanes(row):
    n = row.shape[1]
    ri = lax.broadcasted_iota(jnp.int32, (n, n), 0)
    ci = lax.broadcasted_iota(jnp.int32, (n, n), 1)
    diag = jnp.where(ri == ci, jnp.broadcast_to(row, (n, n)), 0.0)
    return jnp.broadcast_to(jnp.sum(diag, axis=1, keepdims=True), (n, n))


def _gdn_sample_kernel(gate_ref, q_ref, k_ref, v_ref, z_ref, gn_ref, s_ref, stacked_ref, y_ref, so_ref,
                       *, hb, bt):
    del stacked_ref
    hblk = pl.program_id(0)
    tile = pl.program_id(1)
    gn = gn_ref[...]
    samples = range(bt)
    q_col, k_col = {}, {}
    for j in range(hb // 2):
        q = q_ref[:, j * HEAD_K:(j + 1) * HEAD_K]
        k = k_ref[:, j * HEAD_K:(j + 1) * HEAD_K]
        qn = q * (lax.rsqrt(jnp.sum(q * q, axis=-1, keepdims=True) + EPS) * (HEAD_K ** -0.5))
        kn = k * lax.rsqrt(jnp.sum(k * k, axis=-1, keepdims=True) + EPS)
        for b in samples:
            q_col[b, j] = _lanes_to_sublanes(qn[b:b + 1])
            k_col[b, j] = _lanes_to_sublanes(kn[b:b + 1])
    for h in range(hb):
        cols = slice(h * HEAD_V, (h + 1) * HEAD_V)
        j = h // 2
        o_rows = []
        for b in samples:
            base = (tile * bt + b) * (2 * N_V_HEADS) + hblk * hb + h
            beta = gate_ref[base]
            e_g = gate_ref[base + N_V_HEADS]
            s = s_ref[b, h]
            s_k = jnp.sum(s * k_col[b, j], axis=0, keepdims=True)
            delta = beta * (v_ref[b:b + 1, cols] - e_g * s_k)
            s_new = e_g * s + k_col[b, j] * delta
            so_ref[b, h] = s_new
            o_rows.append(jnp.sum(s_new * q_col[b, j], axis=0, keepdims=True))
        o = jnp.concatenate(o_rows, axis=0)
        y_ref[:, cols] = _rms(o, gn) * z_ref[:, cols]


def _gdn_sample_call(gates_flat, qkv, z_act, gdn_norm, state, new_states, li, hb, bt):
    m = qkv.shape[0]
    qw, vw = HEAD_K * hb // 2, HEAD_V * hb
    kb, vb = KEY_DIM // qw, 2 * KEY_DIM // vw
    state_spec = pl.BlockSpec((None, bt, hb, HEAD_K, HEAD_V), lambda h, t: (li, t, h, 0, 0))
    in_specs = [pl.BlockSpec(memory_space=pltpu.SMEM),
                pl.BlockSpec((bt, qw), lambda h, t: (t, h)),
                pl.BlockSpec((bt, qw), lambda h, t: (t, kb + h)),
                pl.BlockSpec((bt, vw), lambda h, t: (t, vb + h)),
                pl.BlockSpec((bt, vw), lambda h, t: (t, h)),
                pl.BlockSpec((None, 1, HEAD_V), lambda h, t: (li, 0, 0)),
                state_spec,
                pl.BlockSpec(memory_space=pl.ANY)]
    return pl.pallas_call(
        functools.partial(_gdn_sample_kernel, hb=hb, bt=bt),
        grid=(N_V_HEADS // hb, m // bt),
        in_specs=in_specs,
        out_specs=[pl.BlockSpec((bt, vw), lambda h, t: (t, h)), state_spec],
        out_shape=[jax.ShapeDtypeStruct((m, VAL_DIM), F32),
                   jax.ShapeDtypeStruct(state.shape, F32)],
        input_output_aliases={len(in_specs) - 1: 1},
        compiler_params=_params("parallel", "parallel"),
        name="gdn_sample",
    )(gates_flat, qkv, qkv, qkv, z_act, gdn_norm, state, new_states)


def _merge_kernel(ya_ref, yb_ref, ga_ref, gb_ref, wa_ref, wb_ref, o_ref):
    a = _dot(ya_ref[...].astype(BF16), wa_ref[...])
    b = _dot(yb_ref[...], wb_ref[...])
    o_ref[...] = (ga_ref[...] * a + gb_ref[...] * b).astype(o_ref.dtype)


def _merge_call(ya, yb, gates, w_gdn_out, w_sconv_out, li, tm, tn):
    m = ya.shape[0]
    gbo = D_MODEL // tn
    return pl.pallas_call(
        _merge_kernel,
        grid=(D_MODEL // tn, m // tm),
        in_specs=[pl.BlockSpec((tm, VAL_DIM), lambda n, i: (i, 0)),
                  pl.BlockSpec((tm, SC_DIM), lambda n, i: (i, 0)),
                  pl.BlockSpec((tm, tn), lambda n, i: (i, n)),
                  pl.BlockSpec((tm, tn), lambda n, i: (i, gbo + n)),
                  pl.BlockSpec((None, VAL_DIM, tn), lambda n, i: (li, 0, n)),
                  pl.BlockSpec((None, SC_DIM, tn), lambda n, i: (li, 0, n))],
        out_specs=pl.BlockSpec((tm, tn), lambda n, i: (i, n)),
        out_shape=jax.ShapeDtypeStruct((m, D_MODEL), BF16),
        compiler_params=_params("parallel", "parallel"),
        name="merge",
    )(ya, yb, gates, gates, w_gdn_out, w_sconv_out)


def _oproj_kernel(x_ref, h_ref, w_ref, g_ref, h_out, n_out):
    h = h_ref[...] + _dot(x_ref[...], w_ref[...])
    h_out[...] = h
    n_out[...] = _rms(h, g_ref[...]).astype(n_out.dtype)


def _oproj_call(x, h, w_o, ffn_norm, li, tm):
    m, d = h.shape
    return pl.pallas_call(
        _oproj_kernel,
        grid=(m // tm,),
        in_specs=[pl.BlockSpec((tm, d), lambda i: (i, 0)),
                  pl.BlockSpec((tm, d), lambda i: (i, 0)),
                  pl.BlockSpec((None, d, d), lambda i: (li, 0, 0)),
                  pl.BlockSpec((None, 1, d), lambda i: (li, 0, 0))],
        out_specs=[pl.BlockSpec((tm, d), lambda i: (i, 0)),
                   pl.BlockSpec((tm, d), lambda i: (i, 0))],
        out_shape=[jax.ShapeDtypeStruct((m, d), F32),
                   jax.ShapeDtypeStruct((m, d), BF16)],
        compiler_params=_params("parallel"),
        name="oproj",
    )(x, h, w_o, ffn_norm)


def _ffn_up_kernel(x_ref, wg_ref, wu_ref, o_ref):
    x = x_ref[...]
    o_ref[...] = (_silu(_dot(x, wg_ref[...])) * _dot(x, wu_ref[...])).astype(o_ref.dtype)


def _ffn_up_call(x, w_gate, w_up, li, tm, tn):
    m, k = x.shape
    return pl.pallas_call(
        _ffn_up_kernel,
        grid=(D_FF // tn, m // tm),
        in_specs=[pl.BlockSpec((tm, k), lambda n, i: (i, 0)),
                  pl.BlockSpec((None, k, tn), lambda n, i: (li, 0, n)),
                  pl.BlockSpec((None, k, tn), lambda n, i: (li, 0, n))],
        out_specs=pl.BlockSpec((tm, tn), lambda n, i: (i, n)),
        out_shape=jax.ShapeDtypeStruct((m, D_FF), BF16),
        compiler_params=_params("parallel", "parallel"),
        name="ffn_up",
    )(x, w_gate, w_up)


def _ffn_down_kernel(x_ref, h_ref, w_ref, o_ref):
    o_ref[...] = h_ref[...] + _dot(x_ref[...], w_ref[...])


def _ffn_down_call(x, h, w_down, li, tm, tn):
    m, k = x.shape
    return pl.pallas_call(
        _ffn_down_kernel,
        grid=(D_MODEL // tn, m // tm),
        in_specs=[pl.BlockSpec((tm, k), lambda n, i: (i, 0)),
                  pl.BlockSpec((tm, tn), lambda n, i: (i, n)),
                  pl.BlockSpec((None, k, tn), lambda n, i: (li, 0, n))],
        out_specs=pl.BlockSpec((tm, tn), lambda n, i: (i, n)),
        out_shape=jax.ShapeDtypeStruct((m, D_MODEL), F32),
        compiler_params=_params("parallel", "parallel"),
        name="ffn_down",
    )(x, h, w_down)


def _ple_kernel(h_ref, p_ref, wg_ref, wp_ref, g_ref, gnext_ref, h_out, n_out):
    h = h_ref[...]
    hn = _rms(h, g_ref[...]).astype(BF16)
    gate = jax.nn.sigmoid(_dot(hn, wg_ref[...]))
    h = h + gate * _dot(p_ref[...].astype(BF16), wp_ref[...])
    h_out[...] = h
    n_out[...] = _rms(h, gnext_ref[...]).astype(n_out.dtype)


def _ple_call(h, p, w_gate, w_proj, ple_norm, next_norm, li, next_li, next_dtype, tm):
    m, d = h.shape
    return pl.pallas_call(
        _ple_kernel,
        grid=(m // tm,),
        in_specs=[pl.BlockSpec((tm, d), lambda i: (i, 0)),
                  pl.BlockSpec((None, tm, PLE_DIM), lambda i: (li, i, 0)),
                  pl.BlockSpec((None, d, d), lambda i: (li, 0, 0)),
                  pl.BlockSpec((None, PLE_DIM, d), lambda i: (li, 0, 0)),
                  pl.BlockSpec((None, 1, d), lambda i: (li, 0, 0)),
                  pl.BlockSpec((None, 1, d), lambda i: (next_li, 0, 0))],
        out_specs=[pl.BlockSpec((tm, d), lambda i: (i, 0)),
                   pl.BlockSpec((tm, d), lambda i: (i, 0))],
        out_shape=[jax.ShapeDtypeStruct((m, d), F32),
                   jax.ShapeDtypeStruct((m, d), next_dtype)],
        compiler_params=_params("parallel"),
        name="ple",
    )(h, p, w_gate, w_proj, ple_norm, next_norm)


def _layer(li, h, xn, p, w, seq, states):
    m = h.shape[0]
    prompt = seq > 1
    tm = 512 if prompt else m
    tm_row = 256 if prompt else m
    hb = GDN_PROMPT_HEADS_PER_STEP if prompt else GDN_SAMPLE_HEADS_PER_STEP

    gates3 = _gates_call(xn, w["ab"], w["a_log"], w["dt_bias"], li, tm)
    z_act = _proj_act_call(xn, w["main"], li, OFF_Z, VAL_DIM, _silu, tm, 1024, "proj_z")
    mix_gates = _proj_act_call(xn, w["rest"], li, REST_G, 2 * D_MODEL, jax.nn.sigmoid, tm, 1024,
                               "proj_mix_gates")

    if prompt:
        qkv, new_qkv = _qkv_prompt_call(xn, w["main"], w["qkv_conv"], li, seq, tm, 512)
        yb, new_sc = _sconv_prompt_call(xn, w["rest"], w["sconv"], li, seq, tm, 512)
        gate_pairs = jnp.concatenate(
            [_gate_pair_layout(gates3[:, :N_V_HEADS], hb),
             _gate_pair_layout(gates3[:, N_V_HEADS:2 * N_V_HEADS], hb)], axis=-1)
        ya, new_gdn = _gdn_prompt_call(qkv, z_act, gate_pairs, w["gdn_norm"], li, seq, hb)
    else:
        s_gdn, s_qkv_t, s_sc_t, new_gdn_stacked = states
        qkv, new_qkv_t = _qkv_sample_call(xn, w["main"], w["qkv_conv"], s_qkv_t, li, 1024)
        yb, new_sc_t = _sconv_sample_call(xn, w["rest"], w["sconv"], s_sc_t, li, 512)
        new_qkv = new_qkv_t.transpose(1, 0, 2)
        new_sc = new_sc_t.transpose(1, 0, 2)
        gates_flat = gates3[:, N_V_HEADS:].reshape(-1)
        ya, new_gdn = _gdn_sample_call(gates_flat, qkv, z_act, w["gdn_norm"], s_gdn, new_gdn_stacked,
                                       li, hb, SAMPLE_TILE)

    mixed = _merge_call(ya, yb, mix_gates, w["gdn_out"], w["sconv_out"], li, tm_row, 1024)
    h2, hn2 = _oproj_call(mixed, h, w["o"], w["ffn_norm"], li, tm_row)
    act = _ffn_up_call(hn2, w["ffn_gate"], w["ffn_up"], li, tm, 512)
    h3 = _ffn_down_call(act, h2, w["ffn_down"], li, tm_row, 1024)
    last = li == DEPTH - 1
    next_norm = w["final_norm"] if last else w["attn_norm"]
    h4, nxt = _ple_call(h3, p, w["ple_gate"], w["ple_proj"], w["ple_norm"], next_norm, li,
                        0 if last else li + 1, F32 if last else BF16, tm_row)
    return h4, nxt, new_gdn, new_qkv, new_sc


def kernel(x_prompt, x_sample, state_gdn, state_qkv_conv, state_short_conv, p_prompt, p_sample, attn_norm, w_in, qkv_conv_w, a_log, dt_bias, gdn_norm, w_gdn_out, sconv_w, w_sconv_out, w_o, ffn_norm, w_ffn_gate, w_ffn_up, w_ffn_down, ple_norm, w_ple_gate, w_ple_proj, final_norm):
    nb, seq, d = x_prompt.shape
    ns = x_sample.shape[0]
    w_in_t = jnp.swapaxes(w_in, 1, 2)
    w = {
        "main": _cast_columns_call(w_in_t, 0, OFF_AB, 2048, 1024),
        "ab": _cast_columns_call(w_in_t, OFF_AB, LANES, 2048, LANES),
        "rest": _cast_columns_call(w_in_t, OFF_REST, w_in.shape[2] - OFF_REST, 2048, 512),
        "qkv_conv": qkv_conv_w,
        "sconv": sconv_w,
        "a_log": a_log[:, None, :],
        "dt_bias": dt_bias[:, None, :],
        "gdn_norm": gdn_norm[:, None, :],
        "gdn_out": w_gdn_out.astype(BF16),
        "sconv_out": w_sconv_out.astype(BF16),
        "o": w_o.astype(BF16),
        "ffn_gate": w_ffn_gate.astype(BF16),
        "ffn_up": w_ffn_up.astype(BF16),
        "ffn_down": w_ffn_down.astype(BF16),
        "ple_gate": w_ple_gate.astype(BF16),
        "ple_proj": w_ple_proj.astype(BF16),
        "attn_norm": attn_norm[:, None, :],
        "ffn_norm": ffn_norm[:, None, :],
        "ple_norm": ple_norm[:, None, :],
        "final_norm": final_norm[None, None, :],
    }
    hp = x_prompt.reshape(nb * seq, d)
    hs = x_sample.reshape(ns, d)
    pp = p_prompt.reshape(DEPTH, nb * seq, PLE_DIM)
    ps = p_sample.reshape(DEPTH, ns, PLE_DIM)
    s_qkv_t = state_qkv_conv.transpose(0, 2, 1, 3)
    s_sc_t = state_short_conv.transpose(0, 2, 1, 3)

    xp = _rmsnorm_call(hp, w["attn_norm"], 0, 256)
    xs = _rmsnorm_call(hs, w["attn_norm"], 0, ns)
    outs_p, outs_s = [], []
    new_gdn_s = jnp.zeros(state_gdn.shape, F32)
    for li in range(DEPTH):
        hp, xp, *st = _layer(li, hp, xp, pp, w, seq, None)
        outs_p.append(st)
        hs, xs, new_gdn_s, *st = _layer(li, hs, xs, ps, w, 1, (state_gdn, s_qkv_t, s_sc_t, new_gdn_s))
        outs_s.append(st)
    y_prompt = xp.reshape(nb, seq, d)
    y_sample = xs.reshape(ns, 1, d)
    stack = lambda outs, j: jnp.stack([o[j] for o in outs])
    return (y_prompt, y_sample, stack(outs_p, 0), stack(outs_p, 1), stack(outs_p, 2),
            new_gdn_s, stack(outs_s, 0), stack(outs_s, 1))
```

```python
import functools

import jax
import jax.numpy as jnp
from jax import lax
from jax.experimental import pallas as pl
from jax.experimental.pallas import tpu as pltpu

F32 = jnp.float32
BF16 = jnp.bfloat16
HI = lax.Precision.HIGHEST

D_MODEL = 2048
DEPTH = 4
N_QK_HEADS = 16
N_V_HEADS = 32
HEAD_K = 128
HEAD_V = 128
KEY_DIM = N_QK_HEADS * HEAD_K
VAL_DIM = N_V_HEADS * HEAD_V
QKV_DIM = 2 * KEY_DIM + VAL_DIM
QKV_CONV = 4
SC_CONV = 3
SC_DIM = D_MODEL
CHUNK = 64
D_FF = 5632
PLE_DIM = 256
EPS = 1e-6

OFF_Z = QKV_DIM
OFF_AB = OFF_Z + VAL_DIM
OFF_REST = OFF_AB + 2 * N_V_HEADS
REST_SB, REST_SC, REST_SH, REST_G = 0, SC_DIM, 2 * SC_DIM, 3 * SC_DIM

VMEM_LIMIT_BYTES = 56 * 1024 * 1024
SUBLANES = 8
LANES = 128
CONV_SUB_COLS = 256

GDN_PROMPT_HEADS_PER_STEP = 32
GDN_SAMPLE_HEADS_PER_STEP = 4
SAMPLE_TILE = 16


def _params(*sem):
    return pltpu.CompilerParams(dimension_semantics=sem, vmem_limit_bytes=VMEM_LIMIT_BYTES)


def _dot(a, b):
    return jnp.dot(a, b, preferred_element_type=F32)


def _dot_nt(a, b):
    return lax.dot_general(a, b, (((1,), (1,)), ((), ())), preferred_element_type=F32)


def _dot_tn(a, b, precision=None):
    return lax.dot_general(a, b, (((0,), (0,)), ((), ())), precision=precision,
                           preferred_element_type=F32)


def _dot_hi(a, b):
    return jnp.dot(a, b, precision=HI, preferred_element_type=F32)


def _rms(x, g):
    return x * lax.rsqrt(jnp.mean(x * x, axis=-1, keepdims=True) + EPS) * g


def _silu(x):
    return x * jax.nn.sigmoid(x)


def _rmsnorm_kernel(x_ref, g_ref, o_ref):
    o_ref[...] = _rms(x_ref[...], g_ref[...]).astype(o_ref.dtype)


def _rmsnorm_call(x, gain, li, tm):
    m, d = x.shape
    return pl.pallas_call(
        _rmsnorm_kernel,
        grid=(m // tm,),
        in_specs=[pl.BlockSpec((tm, d), lambda i: (i, 0)),
                  pl.BlockSpec((None, 1, d), lambda i: (li, 0, 0))],
        out_specs=pl.BlockSpec((tm, d), lambda i: (i, 0)),
        out_shape=jax.ShapeDtypeStruct((m, d), BF16),
        compiler_params=_params("parallel"),
        name="rmsnorm",
    )(x, gain)


def _cast_t_kernel(a_ref, o_ref):
    o_ref[...] = a_ref[...].T.astype(o_ref.dtype)


def _cast_t_shift_kernel(a_ref, b_ref, o_ref, *, shift):
    o_ref[...] = jnp.concatenate([a_ref[shift:, :], b_ref[...]], axis=0).T.astype(o_ref.dtype)


def _cast_columns_call(w_t, col0, width, tk, tn):
    depth, _, k = w_t.shape
    shift = col0 % tn
    base = col0 // tn
    out_spec = pl.BlockSpec((None, tk, tn), lambda l, r, c: (l, r, c))
    a_spec = pl.BlockSpec((None, tn, tk), lambda l, r, c: (l, base + c, r))
    common = dict(grid=(depth, k // tk, width // tn), out_specs=out_spec,
                  out_shape=jax.ShapeDtypeStruct((depth, k, width), BF16),
                  compiler_params=_params("parallel", "parallel", "parallel"))
    if shift == 0:
        return pl.pallas_call(_cast_t_kernel, in_specs=[a_spec], name="cast_w", **common)(w_t)
    per_block = tn // shift
    b_spec = pl.BlockSpec((None, shift, tk), lambda l, r, c: (l, (base + c + 1) * per_block, r))
    return pl.pallas_call(functools.partial(_cast_t_shift_kernel, shift=shift),
                          in_specs=[a_spec, b_spec], name="cast_shift_w", **common)(w_t, w_t)


def _proj_act_kernel(x_ref, w_ref, o_ref, *, act):
    o_ref[...] = act(_dot(x_ref[...], w_ref[...])).astype(o_ref.dtype)


def _proj_act_call(x, w, li, col0, n_out, act, tm, tn, name):
    m, k = x.shape
    cb = col0 // tn
    return pl.pallas_call(
        functools.partial(_proj_act_kernel, act=act),
        grid=(n_out // tn, m // tm),
        in_specs=[pl.BlockSpec((tm, k), lambda n, i: (i, 0)),
                  pl.BlockSpec((None, k, tn), lambda n, i: (li, 0, cb + n))],
        out_specs=pl.BlockSpec((tm, tn), lambda n, i: (i, n)),
        out_shape=jax.ShapeDtypeStruct((m, n_out), F32),
        compiler_params=_params("parallel", "parallel"),
        name=name,
    )(x, w)


def _gates_kernel(x_ref, w_ref, alog_ref, dtb_ref, o_ref):
    r = _dot(x_ref[...], w_ref[...])
    a = r[:, :N_V_HEADS] + dtb_ref[...]
    softplus = jnp.maximum(a, 0.0) + jnp.log(1.0 + jnp.exp(-jnp.abs(a)))
    g = -jnp.exp(alog_ref[...]) * softplus
    o_ref[:, 0:N_V_HEADS] = g
    o_ref[:, N_V_HEADS:2 * N_V_HEADS] = jax.nn.sigmoid(r[:, N_V_HEADS:2 * N_V_HEADS])
    o_ref[:, 2 * N_V_HEADS:3 * N_V_HEADS] = jnp.exp(g)


def _gates_call(x, w_ab, a_log, dt_bias, li, tm):
    m, k = x.shape
    return pl.pallas_call(
        _gates_kernel,
        grid=(m // tm,),
        in_specs=[pl.BlockSpec((tm, k), lambda i: (i, 0)),
                  pl.BlockSpec((None, k, LANES), lambda i: (li, 0, 0)),
                  pl.BlockSpec((None, 1, N_V_HEADS), lambda i: (li, 0, 0)),
                  pl.BlockSpec((None, 1, N_V_HEADS), lambda i: (li, 0, 0))],
        out_specs=pl.BlockSpec((tm, 3 * N_V_HEADS), lambda i: (i, 0)),
        out_shape=jax.ShapeDtypeStruct((m, 3 * N_V_HEADS), F32),
        compiler_params=_params("parallel"),
        name="gates",
    )(x, w_ab, a_log, dt_bias)


def _qkv_prompt_kernel(x_ref, w_ref, cw_ref, act_ref, st_ref, xs_ref, *, tm, tiles_per_seq):
    i = pl.program_id(1)

    @pl.when(i % tiles_per_seq == 0)
    def _():
        xs_ref[0:SUBLANES, :] = jnp.zeros((SUBLANES, xs_ref.shape[1]), F32)

    for c0 in range(0, xs_ref.shape[1], CONV_SUB_COLS):
        cols = slice(c0, c0 + CONV_SUB_COLS)
        x = _dot(x_ref[...], w_ref[:, cols])
        xs_ref[SUBLANES:tm + SUBLANES, cols] = x
        cw = cw_ref[:, cols]
        y = cw[3:4] * x
        for j in range(QKV_CONV - 1):
            lag = QKV_CONV - 1 - j
            y = y + cw[j:j + 1] * xs_ref[SUBLANES - lag:tm + SUBLANES - lag, cols]
        act_ref[:, cols] = _silu(y)
    st_ref[...] = xs_ref[tm + SUBLANES - (QKV_CONV - 1):tm + SUBLANES, :]
    xs_ref[0:SUBLANES, :] = xs_ref[tm:tm + SUBLANES, :]


def _qkv_prompt_call(x, w, conv_w, li, seq, tm, tn):
    m, k = x.shape
    tps = seq // tm
    return pl.pallas_call(
        functools.partial(_qkv_prompt_kernel, tm=tm, tiles_per_seq=tps),
        grid=(QKV_DIM // tn, m // tm),
        in_specs=[pl.BlockSpec((tm, k), lambda n, i: (i, 0)),
                  pl.BlockSpec((None, k, tn), lambda n, i: (li, 0, n)),
                  pl.BlockSpec((None, QKV_CONV, tn), lambda n, i: (li, 0, n))],
        out_specs=[pl.BlockSpec((tm, tn), lambda n, i: (i, n)),
                   pl.BlockSpec((None, QKV_CONV - 1, tn), lambda n, i: (i // tps, 0, n))],
        out_shape=[jax.ShapeDtypeStruct((m, QKV_DIM), F32),
                   jax.ShapeDtypeStruct((m // seq, QKV_CONV - 1, QKV_DIM), F32)],
        scratch_shapes=[pltpu.VMEM((tm + SUBLANES, tn), F32)],
        compiler_params=_params("arbitrary", "arbitrary"),
        name="qkv_conv_prompt",
    )(x, w, conv_w)


def _qkv_sample_kernel(x_ref, w_ref, cw_ref, s_ref, act_ref, st_ref):
    x = _dot(x_ref[...], w_ref[...])
    cw = cw_ref[...]
    y = cw[3:4] * x
    for j in range(QKV_CONV - 1):
        y = y + cw[j:j + 1] * s_ref[j]
    act_ref[...] = _silu(y)
    st_ref[0] = s_ref[1]
    st_ref[1] = s_ref[2]
    st_ref[2] = x


def _qkv_sample_call(x, w, conv_w, state_t, li, tn):
    m, k = x.shape
    return pl.pallas_call(
        _qkv_sample_kernel,
        grid=(QKV_DIM // tn,),
        in_specs=[pl.BlockSpec((m, k), lambda n: (0, 0)),
                  pl.BlockSpec((None, k, tn), lambda n: (li, 0, n)),
                  pl.BlockSpec((None, QKV_CONV, tn), lambda n: (li, 0, n)),
                  pl.BlockSpec((None, QKV_CONV - 1, m, tn), lambda n: (li, 0, 0, n))],
        out_specs=[pl.BlockSpec((m, tn), lambda n: (0, n)),
                   pl.BlockSpec((QKV_CONV - 1, m, tn), lambda n: (0, 0, n))],
        out_shape=[jax.ShapeDtypeStruct((m, QKV_DIM), F32),
                   jax.ShapeDtypeStruct((QKV_CONV - 1, m, QKV_DIM), F32)],
        compiler_params=_params("parallel"),
        name="qkv_conv_sample",
    )(x, w, conv_w, state_t)


def _sconv_prompt_kernel(x_ref, wb_ref, wc_ref, wh_ref, cw_ref, y_ref, st_ref, us_ref, *,
                         tm, tiles_per_seq):
    i = pl.program_id(1)

    @pl.when(i % tiles_per_seq == 0)
    def _():
        us_ref[0:SUBLANES, :] = jnp.zeros((SUBLANES, us_ref.shape[1]), F32)

    cw = cw_ref[...]
    x = x_ref[...]
    u = _dot(x, wc_ref[...]) * _dot(x, wh_ref[...])
    us_ref[SUBLANES:tm + SUBLANES, :] = u
    conv = cw[2:3] * u
    for j in range(SC_CONV - 1):
        lag = SC_CONV - 1 - j
        conv = conv + cw[j:j + 1] * us_ref[SUBLANES - lag:tm + SUBLANES - lag, :]
    y_ref[...] = (_dot(x, wb_ref[...]) * conv).astype(y_ref.dtype)
    st_ref[...] = us_ref[tm + SUBLANES - (SC_CONV - 1):tm + SUBLANES, :]
    us_ref[0:SUBLANES, :] = us_ref[tm:tm + SUBLANES, :]


def _sconv_prompt_call(x, w_rest, conv_w, li, seq, tm, tn):
    m, k = x.shape
    tps = seq // tm
    cb, cc, ch = REST_SB // tn, REST_SC // tn, REST_SH // tn
    return pl.pallas_call(
        functools.partial(_sconv_prompt_kernel, tm=tm, tiles_per_seq=tps),
        grid=(SC_DIM // tn, m // tm),
        in_specs=[pl.BlockSpec((tm, k), lambda n, i: (i, 0)),
                  pl.BlockSpec((None, k, tn), lambda n, i: (li, 0, cb + n)),
                  pl.BlockSpec((None, k, tn), lambda n, i: (li, 0, cc + n)),
                  pl.BlockSpec((None, k, tn), lambda n, i: (li, 0, ch + n)),
                  pl.BlockSpec((None, SC_CONV, tn), lambda n, i: (li, 0, n))],
        out_specs=[pl.BlockSpec((tm, tn), lambda n, i: (i, n)),
                   pl.BlockSpec((None, SC_CONV - 1, tn), lambda n, i: (i // tps, 0, n))],
        out_shape=[jax.ShapeDtypeStruct((m, SC_DIM), BF16),
                   jax.ShapeDtypeStruct((m // seq, SC_CONV - 1, SC_DIM), F32)],
        scratch_shapes=[pltpu.VMEM((tm + SUBLANES, tn), F32)],
        compiler_params=_params("arbitrary", "arbitrary"),
        name="sconv_prompt",
    )(x, w_rest, w_rest, w_rest, conv_w)


def _sconv_sample_kernel(x_ref, wb_ref, wc_ref, wh_ref, cw_ref, s_ref, y_ref, st_ref):
    x = x_ref[...]
    u = _dot(x, wc_ref[...]) * _dot(x, wh_ref[...])
    cw = cw_ref[...]
    conv = cw[2:3] * u
    for j in range(SC_CONV - 1):
        conv = conv + cw[j:j + 1] * s_ref[j]
    y_ref[...] = (_dot(x, wb_ref[...]) * conv).astype(y_ref.dtype)
    st_ref[0] = s_ref[1]
    st_ref[1] = u


def _sconv_sample_call(x, w_rest, conv_w, state_t, li, tn):
    m, k = x.shape
    cb, cc, ch = REST_SB // tn, REST_SC // tn, REST_SH // tn
    return pl.pallas_call(
        _sconv_sample_kernel,
        grid=(SC_DIM // tn,),
        in_specs=[pl.BlockSpec((m, k), lambda n: (0, 0)),
                  pl.BlockSpec((None, k, tn), lambda n: (li, 0, cb + n)),
                  pl.BlockSpec((None, k, tn), lambda n: (li, 0, cc + n)),
                  pl.BlockSpec((None, k, tn), lambda n: (li, 0, ch + n)),
                  pl.BlockSpec((None, SC_CONV, tn), lambda n: (li, 0, n)),
                  pl.BlockSpec((None, SC_CONV - 1, m, tn), lambda n: (li, 0, 0, n))],
        out_specs=[pl.BlockSpec((m, tn), lambda n: (0, n)),
                   pl.BlockSpec((SC_CONV - 1, m, tn), lambda n: (0, 0, n))],
        out_shape=[jax.ShapeDtypeStruct((m, SC_DIM), BF16),
                   jax.ShapeDtypeStruct((SC_CONV - 1, m, SC_DIM), F32)],
        compiler_params=_params("parallel"),
        name="sconv_sample",
    )(x, w_rest, w_rest, w_rest, conv_w, state_t)


def _gdn_prompt_kernel(q_ref, k_ref, v_ref, z_ref, gp_ref, gn_ref, y_ref, s_ref, *, hb):
    c = CHUNK
    c2 = 2 * c
    npairs = hb // 2

    @pl.when(pl.program_id(2) == 0)
    def _():
        s_ref[...] = jnp.zeros(s_ref.shape, F32)

    ri = lax.broadcasted_iota(jnp.int32, (c2, c2), 0)
    ci = lax.broadcasted_iota(jnp.int32, (c2, c2), 1)
    r_in_rblk = (ri & (c - 1)).astype(jnp.uint32)
    c_in_rblk = (ci - (ri - (ri & (c - 1)))).astype(jnp.uint32)
    r_in_cblk = (ri - (ci - (ci & (c - 1)))).astype(jnp.uint32)
    c_in_cblk = (ci & (c - 1)).astype(jnp.uint32)
    lower = c_in_rblk <= r_in_rblk
    strict = c_in_rblk < r_in_rblk
    same = c_in_rblk < jnp.uint32(c)
    upper = r_in_cblk <= c_in_cblk

    gp = gp_ref[...]
    g = gp[:, :npairs]
    beta = gp[:, npairs:]
    g_col = _dot_hi(lower.astype(F32), g)
    g_row = _dot_tn(g, upper.astype(F32), precision=HI)
    g_last = _dot_hi(same.astype(F32), g)
    e_g = jnp.exp(g_col)
    e_rest = jnp.exp(g_last - g_col)
    e_last = jnp.exp(g_last)
    beta_e_g = beta * e_g
    gn = gn_ref[...]

    pairs = range(npairs)
    heads = [(p, r) for p in pairs for r in range(2)]
    k_st, q_st, k_b, kq = [], [], [], []
    for p in pairs:
        q = q_ref[:, p * HEAD_K:(p + 1) * HEAD_K]
        k = k_ref[:, p * HEAD_K:(p + 1) * HEAD_K]
        qn = q * (lax.rsqrt(jnp.sum(q * q, axis=-1, keepdims=True) + EPS) * (HEAD_K ** -0.5))
        kn = k * lax.rsqrt(jnp.sum(k * k, axis=-1, keepdims=True) + EPS)
        k_st.append(jnp.concatenate([kn, kn], axis=0))
        q_st.append(jnp.concatenate([qn, qn], axis=0))
        k_b.append(k_st[p].astype(BF16))
        kq.append(_dot_nt(jnp.concatenate([k_b[p], q_st[p].astype(BF16)], axis=0), k_b[p]))
    decay = [jnp.exp(jnp.where(lower, g_col[:, p:p + 1] - g_row[p:p + 1, :], -jnp.inf)) for p in pairs]
    a = [jnp.where(strict, beta[:, p:p + 1] * kq[p][:c2] * decay[p], 0.0) for p in pairs]
    qkd_b = [(kq[p][c2:] * decay[p]).astype(BF16) for p in pairs]
    a_b = [x.astype(BF16) for x in a]
    corr = [-x for x in a]
    pw = [_dot(x, x) for x in a_b]
    span = 2
    while span < c:
        pw_b = [x.astype(BF16) for x in pw]
        corr = [corr[p] + pw[p] + _dot(corr[p].astype(BF16), pw_b[p]) for p in pairs]
        span *= 2
        if span < c:
            pw = [_dot(x, x) for x in pw_b]
    sol = []
    for p in pairs:
        v_st = jnp.concatenate([v_ref[:, (2 * p + r) * HEAD_V:(2 * p + r + 1) * HEAD_V]
                                for r in range(2)], axis=0)
        rhs = jnp.concatenate([v_st * beta[:, p:p + 1], k_st[p] * beta_e_g[:, p:p + 1]], axis=1)
        sol.append(rhs + _dot(corr[p].astype(BF16), rhs.astype(BF16)))
    qe_st = [q_st[p] * e_g[:, p:p + 1] for p in pairs]
    kd_b = [(k_st[p] * e_rest[:, p:p + 1]).astype(BF16) for p in pairs]
    s_old = [s_ref[2 * p + r] for p, r in heads]
    ws_qs = [_dot(jnp.concatenate([sol[p][r * c:(r + 1) * c, HEAD_V:], qe_st[p][r * c:(r + 1) * c]],
                                  axis=0).astype(BF16), s_old[2 * p + r].astype(BF16))
             for p, r in heads]
    u_new = [sol[p][r * c:(r + 1) * c, :HEAD_V] - ws_qs[2 * p + r][:c] for p, r in heads]
    u_new_b = [x.astype(BF16) for x in u_new]
    for p, r in heads:
        h = 2 * p + r
        s_ref[h] = (s_old[h] * e_last[r * c:r * c + 1, p:p + 1]
                    + _dot_tn(kd_b[p][r * c:(r + 1) * c], u_new_b[h]))
    for p in pairs:
        o_st = (jnp.concatenate([ws_qs[2 * p][c:], ws_qs[2 * p + 1][c:]], axis=0)
                + _dot(qkd_b[p], jnp.concatenate([u_new_b[2 * p], u_new_b[2 * p + 1]], axis=0)))
        y_st = _rms(o_st, gn)
        for r in range(2):
            cols = slice((2 * p + r) * HEAD_V, (2 * p + r + 1) * HEAD_V)
            y_ref[:, cols] = (y_st[r * c:(r + 1) * c] * z_ref[:, cols]).astype(y_ref.dtype)


def _gate_pair_layout(x, hb):
    rows, heads = x.shape
    x = x.reshape(rows // CHUNK, CHUNK, heads // hb, hb // 2, 2)
    return x.transpose(2, 0, 4, 1, 3).reshape(heads // hb, rows // CHUNK, 2 * CHUNK, hb // 2)


def _gdn_prompt_call(qkv, z_act, gate_pairs, gdn_norm, li, seq, hb):
    m = qkv.shape[0]
    nb, nc = m // seq, seq // CHUNK
    qw, vw = HEAD_K * hb // 2, HEAD_V * hb
    kb, vb = KEY_DIM // qw, 2 * KEY_DIM // vw
    return pl.pallas_call(
        functools.partial(_gdn_prompt_kernel, hb=hb),
        grid=(nb, N_V_HEADS // hb, nc),
        in_specs=[pl.BlockSpec((CHUNK, qw), lambda b, h, c: (b * nc + c, h)),
                  pl.BlockSpec((CHUNK, qw), lambda b, h, c: (b * nc + c, kb + h)),
                  pl.BlockSpec((CHUNK, vw), lambda b, h, c: (b * nc + c, vb + h)),
                  pl.BlockSpec((CHUNK, vw), lambda b, h, c: (b * nc + c, h)),
                  pl.BlockSpec((None, None, 2 * CHUNK, hb), lambda b, h, c: (h, b * nc + c, 0, 0)),
                  pl.BlockSpec((None, 1, HEAD_V), lambda b, h, c: (li, 0, 0))],
        out_specs=[pl.BlockSpec((CHUNK, vw), lambda b, h, c: (b * nc + c, h)),
                   pl.BlockSpec((None, hb, HEAD_K, HEAD_V), lambda b, h, c: (b, h, 0, 0))],
        out_shape=[jax.ShapeDtypeStruct((m, VAL_DIM), BF16),
                   jax.ShapeDtypeStruct((nb, N_V_HEADS, HEAD_K, HEAD_V), F32)],
        compiler_params=_params("parallel", "parallel", "arbitrary"),
        name="gdn_prompt",
    )(qkv, qkv, qkv, z_act, gate_pairs, gdn_norm)


def _lanes_to_sublanes(row):
    n = row.shape[1]
    ri = lax.broadcasted_iota(jnp.int32, (n, n), 0)
    ci = lax.broadcasted_iota(jnp.int32, (n, n), 1)
    diag = jnp.where(ri == ci, jnp.broadcast_to(row, (n, n)), 0.0)
    return jnp.broadcast_to(jnp.sum(diag, axis=1, keepdims=True), (n, n))


def _gdn_sample_kernel(gate_ref, q_ref, k_ref, v_ref, z_ref, gn_ref, s_ref, stacked_ref, y_ref, so_ref,
                       *, hb, bt):
    del stacked_ref
    hblk = pl.program_id(0)
    tile = pl.program_id(1)
    gn = gn_ref[...]
    samples = range(bt)
    q_col, k_col = {}, {}
    for j in range(hb // 2):
        q = q_ref[:, j * HEAD_K:(j + 1) * HEAD_K]
        k = k_ref[:, j * HEAD_K:(j + 1) * HEAD_K]
        qn = q * (lax.rsqrt(jnp.sum(q * q, axis=-1, keepdims=True) + EPS) * (HEAD_K ** -0.5))
        kn = k * lax.rsqrt(jnp.sum(k * k, axis=-1, keepdims=True) + EPS)
        for b in samples:
            q_col[b, j] = _lanes_to_sublanes(qn[b:b + 1])
            k_col[b, j] = _lanes_to_sublanes(kn[b:b + 1])
    for h in range(hb):
        cols = slice(h * HEAD_V, (h + 1) * HEAD_V)
        j = h // 2
        o_rows = []
        for b in samples:
            base = (tile * bt + b) * (2 * N_V_HEADS) + hblk * hb + h
            beta = gate_ref[base]
            e_g = gate_ref[base + N_V_HEADS]
            s = s_ref[b, h]
            s_k = jnp.sum(s * k_col[b, j], axis=0, keepdims=True)
            delta = beta * (v_ref[b:b + 1, cols] - e_g * s_k)
            s_new = e_g * s + k_col[b, j] * delta
            so_ref[b, h] = s_new
            o_rows.append(jnp.sum(s_new * q_col[b, j], axis=0, keepdims=True))
        o = jnp.concatenate(o_rows, axis=0)
        y_ref[:, cols] = _rms(o, gn) * z_ref[:, cols]


def _gdn_sample_call(gates_flat, qkv, z_act, gdn_norm, state, new_states, li, hb, bt):
    m = qkv.shape[0]
    qw, vw = HEAD_K * hb // 2, HEAD_V * hb
    kb, vb = KEY_DIM // qw, 2 * KEY_DIM // vw
    state_spec = pl.BlockSpec((None, bt, hb, HEAD_K, HEAD_V), lambda h, t: (li, t, h, 0, 0))
    in_specs = [pl.BlockSpec(memory_space=pltpu.SMEM),
                pl.BlockSpec((bt, qw), lambda h, t: (t, h)),
                pl.BlockSpec((bt, qw), lambda h, t: (t, kb + h)),
                pl.BlockSpec((bt, vw), lambda h, t: (t, vb + h)),
                pl.BlockSpec((bt, vw), lambda h, t: (t, h)),
                pl.BlockSpec((None, 1, HEAD_V), lambda h, t: (li, 0, 0)),
                state_spec,
                pl.BlockSpec(memory_space=pl.ANY)]
    return pl.pallas_call(
        functools.partial(_gdn_sample_kernel, hb=hb, bt=bt),
        grid=(N_V_HEADS // hb, m // bt),
        in_specs=in_specs,
        out_specs=[pl.BlockSpec((bt, vw), lambda h, t: (t, h)), state_spec],
        out_shape=[jax.ShapeDtypeStruct((m, VAL_DIM), F32),
                   jax.ShapeDtypeStruct(state.shape, F32)],
        input_output_aliases={len(in_specs) - 1: 1},
        compiler_params=_params("parallel", "parallel"),
        name="gdn_sample",
    )(gates_flat, qkv, qkv, qkv, z_act, gdn_norm, state, new_states)


def _merge_kernel(ya_ref, yb_ref, ga_ref, gb_ref, wa_ref, wb_ref, o_ref):
    a = _dot(ya_ref[...].astype(BF16), wa_ref[...])
    b = _dot(yb_ref[...], wb_ref[...])
    o_ref[...] = (ga_ref[...] * a + gb_ref[...] * b).astype(o_ref.dtype)


def _merge_call(ya, yb, gates, w_gdn_out, w_sconv_out, li, tm, tn):
    m = ya.shape[0]
    gbo = D_MODEL // tn
    return pl.pallas_call(
        _merge_kernel,
        grid=(D_MODEL // tn, m // tm),
        in_specs=[pl.BlockSpec((tm, VAL_DIM), lambda n, i: (i, 0)),
                  pl.BlockSpec((tm, SC_DIM), lambda n, i: (i, 0)),
                  pl.BlockSpec((tm, tn), lambda n, i: (i, n)),
                  pl.BlockSpec((tm, tn), lambda n, i: (i, gbo + n)),
                  pl.BlockSpec((None, VAL_DIM, tn), lambda n, i: (li, 0, n)),
                  pl.BlockSpec((None, SC_DIM, tn), lambda n, i: (li, 0, n))],
        out_specs=pl.BlockSpec((tm, tn), lambda n, i: (i, n)),
        out_shape=jax.ShapeDtypeStruct((m, D_MODEL), BF16),
        compiler_params=_params("parallel", "parallel"),
        name="merge",
    )(ya, yb, gates, gates, w_gdn_out, w_sconv_out)


def _oproj_kernel(x_ref, h_ref, w_ref, g_ref, h_out, n_out):
    h = h_ref[...] + _dot(x_ref[...], w_ref[...])
    h_out[...] = h
    n_out[...] = _rms(h, g_ref[...]).astype(n_out.dtype)


def _oproj_call(x, h, w_o, ffn_norm, li, tm):
    m, d = h.shape
    return pl.pallas_call(
        _oproj_kernel,
        grid=(m // tm,),
        in_specs=[pl.BlockSpec((tm, d), lambda i: (i, 0)),
                  pl.BlockSpec((tm, d), lambda i: (i, 0)),
                  pl.BlockSpec((None, d, d), lambda i: (li, 0, 0)),
                  pl.BlockSpec((None, 1, d), lambda i: (li, 0, 0))],
        out_specs=[pl.BlockSpec((tm, d), lambda i: (i, 0)),
                   pl.BlockSpec((tm, d), lambda i: (i, 0))],
        out_shape=[jax.ShapeDtypeStruct((m, d), F32),
                   jax.ShapeDtypeStruct((m, d), BF16)],
        compiler_params=_params("parallel"),
        name="oproj",
    )(x, h, w_o, ffn_norm)


def _ffn_up_kernel(x_ref, wg_ref, wu_ref, o_ref):
    x = x_ref[...]
    o_ref[...] = (_silu(_dot(x, wg_ref[...])) * _dot(x, wu_ref[...])).astype(o_ref.dtype)


def _ffn_up_call(x, w_gate, w_up, li, tm, tn):
    m, k = x.shape
    return pl.pallas_call(
        _ffn_up_kernel,
        grid=(D_FF // tn, m // tm),
        in_specs=[pl.BlockSpec((tm, k), lambda n, i: (i, 0)),
                  pl.BlockSpec((None, k, tn), lambda n, i: (li, 0, n)),
                  pl.BlockSpec((None, k, tn), lambda n, i: (li, 0, n))],
        out_specs=pl.BlockSpec((tm, tn), lambda n, i: (i, n)),
        out_shape=jax.ShapeDtypeStruct((m, D_FF), BF16),
        compiler_params=_params("parallel", "parallel"),
        name="ffn_up",
    )(x, w_gate, w_up)


def _ffn_down_kernel(x_ref, h_ref, w_ref, o_ref):
    o_ref[...] = h_ref[...] + _dot(x_ref[...], w_ref[...])


def _ffn_down_call(x, h, w_down, li, tm, tn):
    m, k = x.shape
    return pl.pallas_call(
        _ffn_down_kernel,
        grid=(D_MODEL // tn, m // tm),
        in_specs=[pl.BlockSpec((tm, k), lambda n, i: (i, 0)),
                  pl.BlockSpec((tm, tn), lambda n, i: (i, n)),
                  pl.BlockSpec((None, k, tn), lambda n, i: (li, 0, n))],
        out_specs=pl.BlockSpec((tm, tn), lambda n, i: (i, n)),
        out_shape=jax.ShapeDtypeStruct((m, D_MODEL), F32),
        compiler_params=_params("parallel", "parallel"),
        name="ffn_down",
    )(x, h, w_down)


def _ple_kernel(h_ref, p_ref, wg_ref, wp_ref, g_ref, gnext_ref, h_out, n_out):
    h = h_ref[...]
    hn = _rms(h, g_ref[...]).astype(BF16)
    gate = jax.nn.sigmoid(_dot(hn, wg_ref[...]))
    h = h + gate * _dot(p_ref[...].astype(BF16), wp_ref[...])
    h_out[...] = h
    n_out[...] = _rms(h, gnext_ref[...]).astype(n_out.dtype)


def _ple_call(h, p, w_gate, w_proj, ple_norm, next_norm, li, next_li, next_dtype, tm):
    m, d = h.shape
    return pl.pallas_call(
        _ple_kernel,
        grid=(m // tm,),
        in_specs=[pl.BlockSpec((tm, d), lambda i: (i, 0)),
                  pl.BlockSpec((None, tm, PLE_DIM), lambda i: (li, i, 0)),
                  pl.BlockSpec((None, d, d), lambda i: (li, 0, 0)),
                  pl.BlockSpec((None, PLE_DIM, d), lambda i: (li, 0, 0)),
                  pl.BlockSpec((None, 1, d), lambda i: (li, 0, 0)),
                  pl.BlockSpec((None, 1, d), lambda i: (next_li, 0, 0))],
        out_specs=[pl.BlockSpec((tm, d), lambda i: (i, 0)),
                   pl.BlockSpec((tm, d), lambda i: (i, 0))],
        out_shape=[jax.ShapeDtypeStruct((m, d), F32),
                   jax.ShapeDtypeStruct((m, d), next_dtype)],
        compiler_params=_params("parallel"),
        name="ple",
    )(h, p, w_gate, w_proj, ple_norm, next_norm)


def _layer(li, h, xn, p, w, seq, states):
    m = h.shape[0]
    prompt = seq > 1
    tm = 512 if prompt else m
    tm_row = 256 if prompt else m
    hb = GDN_PROMPT_HEADS_PER_STEP if prompt else GDN_SAMPLE_HEADS_PER_STEP

    gates3 = _gates_call(xn, w["ab"], w["a_log"], w["dt_bias"], li, tm)
    z_act = _proj_act_call(xn, w["main"], li, OFF_Z, VAL_DIM, _silu, tm, 2048, "proj_z")
    mix_gates = _proj_act_call(xn, w["rest"], li, REST_G, 2 * D_MODEL, jax.nn.sigmoid, tm, 2048,
                               "proj_mix_gates")

    if prompt:
        qkv, new_qkv = _qkv_prompt_call(xn, w["main"], w["qkv_conv"], li, seq, tm, 1024)
        yb, new_sc = _sconv_prompt_call(xn, w["rest"], w["sconv"], li, seq, tm, 512)
        gate_pairs = jnp.concatenate(
            [_gate_pair_layout(gates3[:, :N_V_HEADS], hb),
             _gate_pair_layout(gates3[:, N_V_HEADS:2 * N_V_HEADS], hb)], axis=-1)
        ya, new_gdn = _gdn_prompt_call(qkv, z_act, gate_pairs, w["gdn_norm"], li, seq, hb)
    else:
        s_gdn, s_qkv_t, s_sc_t, new_gdn_stacked = states
        qkv, new_qkv_t = _qkv_sample_call(xn, w["main"], w["qkv_conv"], s_qkv_t, li, 1024)
        yb, new_sc_t = _sconv_sample_call(xn, w["rest"], w["sconv"], s_sc_t, li, 512)
        new_qkv = new_qkv_t.transpose(1, 0, 2)
        new_sc = new_sc_t.transpose(1, 0, 2)
        gates_flat = gates3[:, N_V_HEADS:].reshape(-1)
        ya, new_gdn = _gdn_sample_call(gates_flat, qkv, z_act, w["gdn_norm"], s_gdn, new_gdn_stacked,
                                       li, hb, SAMPLE_TILE)

    mixed = _merge_call(ya, yb, mix_gates, w["gdn_out"], w["sconv_out"], li, tm_row, 1024)
    h2, hn2 = _oproj_call(mixed, h, w["o"], w["ffn_norm"], li, tm)
    act = _ffn_up_call(hn2, w["ffn_gate"], w["ffn_up"], li, tm, D_FF // 4)
    h3 = _ffn_down_call(act, h2, w["ffn_down"], li, tm, 1024)
    last = li == DEPTH - 1
    next_norm = w["final_norm"] if last else w["attn_norm"]
    h4, nxt = _ple_call(h3, p, w["ple_gate"], w["ple_proj"], w["ple_norm"], next_norm, li,
                        0 if last else li + 1, F32 if last else BF16, tm)
    return h4, nxt, new_gdn, new_qkv, new_sc


def kernel(x_prompt, x_sample, state_gdn, state_qkv_conv, state_short_conv, p_prompt, p_sample, attn_norm, w_in, qkv_conv_w, a_log, dt_bias, gdn_norm, w_gdn_out, sconv_w, w_sconv_out, w_o, ffn_norm, w_ffn_gate, w_ffn_up, w_ffn_down, ple_norm, w_ple_gate, w_ple_proj, final_norm):
    nb, seq, d = x_prompt.shape
    ns = x_sample.shape[0]
    w_in_t = jnp.swapaxes(w_in, 1, 2)
    w = {
        "main": _cast_columns_call(w_in_t, 0, OFF_AB, 2048, 1024),
        "ab": _cast_columns_call(w_in_t, OFF_AB, LANES, 2048, LANES),
        "rest": _cast_columns_call(w_in_t, OFF_REST, w_in.shape[2] - OFF_REST, 2048, 512),
        "qkv_conv": qkv_conv_w,
        "sconv": sconv_w,
        "a_log": a_log[:, None, :],
        "dt_bias": dt_bias[:, None, :],
        "gdn_norm": gdn_norm[:, None, :],
        "gdn_out": w_gdn_out.astype(BF16),
        "sconv_out": w_sconv_out.astype(BF16),
        "o": w_o.astype(BF16),
        "ffn_gate": w_ffn_gate.astype(BF16),
        "ffn_up": w_ffn_up.astype(BF16),
        "ffn_down": w_ffn_down.astype(BF16),
        "ple_gate": w_ple_gate.astype(BF16),
        "ple_proj": w_ple_proj.astype(BF16),
        "attn_norm": attn_norm[:, None, :],
        "ffn_norm": ffn_norm[:, None, :],
        "ple_norm": ple_norm[:, None, :],
        "final_norm": final_norm[None, None, :],
    }
    hp = x_prompt.reshape(nb * seq, d)
    hs = x_sample.reshape(ns, d)
    pp = p_prompt.reshape(DEPTH, nb * seq, PLE_DIM)
    ps = p_sample.reshape(DEPTH, ns, PLE_DIM)
    s_qkv_t = state_qkv_conv.transpose(0, 2, 1, 3)
    s_sc_t = state_short_conv.transpose(0, 2, 1, 3)

    xp = _rmsnorm_call(hp, w["attn_norm"], 0, 256)
    xs = _rmsnorm_call(hs, w["attn_norm"], 0, ns)
    outs_p, outs_s = [], []
    new_gdn_s = jnp.zeros(state_gdn.shape, F32)
    for li in range(DEPTH):
        hp, xp, *st = _layer(li, hp, xp, pp, w, seq, None)
        outs_p.append(st)
        hs, xs, new_gdn_s, *st = _layer(li, hs, xs, ps, w, 1, (state_gdn, s_qkv_t, s_sc_t, new_gdn_s))
        outs_s.append(st)
    y_prompt = xp.reshape(nb, seq, d)
    y_sample = xs.reshape(ns, 1, d)
    stack = lambda outs, j: jnp.stack([o[j] for o in outs])
    return (y_prompt, y_sample, stack(outs_p, 0), stack(outs_p, 1), stack(outs_p, 2),
            new_gdn_s, stack(outs_s, 0), stack(outs_s, 1))
```

```python
import functools

import jax
import jax.numpy as jnp
from jax import lax
from jax.experimental import pallas as pl
from jax.experimental.pallas import tpu as pltpu

F32 = jnp.float32
BF16 = jnp.bfloat16
HI = lax.Precision.HIGHEST

D_MODEL = 2048
DEPTH = 4
N_QK_HEADS = 16
N_V_HEADS = 32
HEAD_K = 128
HEAD_V = 128
KEY_DIM = N_QK_HEADS * HEAD_K
VAL_DIM = N_V_HEADS * HEAD_V
QKV_DIM = 2 * KEY_DIM + VAL_DIM
QKV_CONV = 4
SC_CONV = 3
SC_DIM = D_MODEL
CHUNK = 64
D_FF = 5632
PLE_DIM = 256
EPS = 1e-6

OFF_Z = QKV_DIM
OFF_AB = OFF_Z + VAL_DIM
OFF_REST = OFF_AB + 2 * N_V_HEADS
REST_SB, REST_SC, REST_SH, REST_G = 0, SC_DIM, 2 * SC_DIM, 3 * SC_DIM

VMEM_LIMIT_BYTES = 56 * 1024 * 1024
SUBLANES = 8
LANES = 128
CONV_SUB_COLS = 256

GDN_PROMPT_HEADS_PER_STEP = 32
GDN_SAMPLE_HEADS_PER_STEP = 4
SAMPLE_TILE = 16


def _params(*sem):
    return pltpu.CompilerParams(dimension_semantics=sem, vmem_limit_bytes=VMEM_LIMIT_BYTES)


def _dot(a, b):
    return jnp.dot(a, b, preferred_element_type=F32)


def _dot_nt(a, b):
    return lax.dot_general(a, b, (((1,), (1,)), ((), ())), preferred_element_type=F32)


def _dot_tn(a, b, precision=None):
    return lax.dot_general(a, b, (((0,), (0,)), ((), ())), precision=precision,
                           preferred_element_type=F32)


def _dot_hi(a, b):
    return jnp.dot(a, b, precision=HI, preferred_element_type=F32)


def _rms(x, g):
    return x * lax.rsqrt(jnp.mean(x * x, axis=-1, keepdims=True) + EPS) * g


def _silu(x):
    return x * jax.nn.sigmoid(x)


def _rmsnorm_kernel(x_ref, g_ref, o_ref):
    o_ref[...] = _rms(x_ref[...], g_ref[...]).astype(o_ref.dtype)


def _rmsnorm_call(x, gain, li, tm):
    m, d = x.shape
    return pl.pallas_call(
        _rmsnorm_kernel,
        grid=(m // tm,),
        in_specs=[pl.BlockSpec((tm, d), lambda i: (i, 0)),
                  pl.BlockSpec((None, 1, d), lambda i: (li, 0, 0))],
        out_specs=pl.BlockSpec((tm, d), lambda i: (i, 0)),
        out_shape=jax.ShapeDtypeStruct((m, d), BF16),
        compiler_params=_params("parallel"),
        name="rmsnorm",
    )(x, gain)


def _cast_t_kernel(a_ref, o_ref):
    o_ref[...] = a_ref[...].T.astype(o_ref.dtype)


def _cast_t_shift_kernel(a_ref, b_ref, o_ref, *, shift):
    o_ref[...] = jnp.concatenate([a_ref[shift:, :], b_ref[...]], axis=0).T.astype(o_ref.dtype)


def _cast_columns_call(w_t, col0, width, tk, tn):
    depth, _, k = w_t.shape
    shift = col0 % tn
    base = col0 // tn
    out_spec = pl.BlockSpec((None, tk, tn), lambda l, r, c: (l, r, c))
    a_spec = pl.BlockSpec((None, tn, tk), lambda l, r, c: (l, base + c, r))
    common = dict(grid=(depth, k // tk, width // tn), out_specs=out_spec,
                  out_shape=jax.ShapeDtypeStruct((depth, k, width), BF16),
                  compiler_params=_params("parallel", "parallel", "parallel"))
    if shift == 0:
        return pl.pallas_call(_cast_t_kernel, in_specs=[a_spec], name="cast_w", **common)(w_t)
    per_block = tn // shift
    b_spec = pl.BlockSpec((None, shift, tk), lambda l, r, c: (l, (base + c + 1) * per_block, r))
    return pl.pallas_call(functools.partial(_cast_t_shift_kernel, shift=shift),
                          in_specs=[a_spec, b_spec], name="cast_shift_w", **common)(w_t, w_t)


def _proj_act_kernel(x_ref, w_ref, o_ref, *, act):
    o_ref[...] = act(_dot(x_ref[...], w_ref[...])).astype(o_ref.dtype)


def _proj_act_call(x, w, li, col0, n_out, act, tm, tn, name):
    m, k = x.shape
    cb = col0 // tn
    return pl.pallas_call(
        functools.partial(_proj_act_kernel, act=act),
        grid=(n_out // tn, m // tm),
        in_specs=[pl.BlockSpec((tm, k), lambda n, i: (i, 0)),
                  pl.BlockSpec((None, k, tn), lambda n, i: (li, 0, cb + n))],
        out_specs=pl.BlockSpec((tm, tn), lambda n, i: (i, n)),
        out_shape=jax.ShapeDtypeStruct((m, n_out), F32),
        compiler_params=_params("parallel", "parallel"),
        name=name,
    )(x, w)


def _gates_kernel(x_ref, w_ref, alog_ref, dtb_ref, o_ref):
    r = _dot(x_ref[...], w_ref[...])
    a = r[:, :N_V_HEADS] + dtb_ref[...]
    softplus = jnp.maximum(a, 0.0) + jnp.log(1.0 + jnp.exp(-jnp.abs(a)))
    g = -jnp.exp(alog_ref[...]) * softplus
    o_ref[:, 0:N_V_HEADS] = g
    o_ref[:, N_V_HEADS:2 * N_V_HEADS] = jax.nn.sigmoid(r[:, N_V_HEADS:2 * N_V_HEADS])
    o_ref[:, 2 * N_V_HEADS:3 * N_V_HEADS] = jnp.exp(g)


def _gates_call(x, w_ab, a_log, dt_bias, li, tm):
    m, k = x.shape
    return pl.pallas_call(
        _gates_kernel,
        grid=(m // tm,),
        in_specs=[pl.BlockSpec((tm, k), lambda i: (i, 0)),
                  pl.BlockSpec((None, k, LANES), lambda i: (li, 0, 0)),
                  pl.BlockSpec((None, 1, N_V_HEADS), lambda i: (li, 0, 0)),
                  pl.BlockSpec((None, 1, N_V_HEADS), lambda i: (li, 0, 0))],
        out_specs=pl.BlockSpec((tm, 3 * N_V_HEADS), lambda i: (i, 0)),
        out_shape=jax.ShapeDtypeStruct((m, 3 * N_V_HEADS), F32),
        compiler_params=_params("parallel"),
        name="gates",
    )(x, w_ab, a_log, dt_bias)


def _qkv_prompt_kernel(x_ref, w_ref, cw_ref, act_ref, st_ref, xs_ref, *, tm, tiles_per_seq):
    i = pl.program_id(1)

    @pl.when(i % tiles_per_seq == 0)
    def _():
        xs_ref[0:SUBLANES, :] = jnp.zeros((SUBLANES, xs_ref.shape[1]), F32)

    for c0 in range(0, xs_ref.shape[1], CONV_SUB_COLS):
        cols = slice(c0, c0 + CONV_SUB_COLS)
        x = _dot(x_ref[...], w_ref[:, cols])
        xs_ref[SUBLANES:tm + SUBLANES, cols] = x
        cw = cw_ref[:, cols]
        y = cw[3:4] * x
        for j in range(QKV_CONV - 1):
            lag = QKV_CONV - 1 - j
            y = y + cw[j:j + 1] * xs_ref[SUBLANES - lag:tm + SUBLANES - lag, cols]
        act_ref[:, cols] = _silu(y)
    st_ref[...] = xs_ref[tm + SUBLANES - (QKV_CONV - 1):tm + SUBLANES, :]
    xs_ref[0:SUBLANES, :] = xs_ref[tm:tm + SUBLANES, :]


def _qkv_prompt_call(x, w, conv_w, li, seq, tm, tn):
    m, k = x.shape
    tps = seq // tm
    return pl.pallas_call(
        functools.partial(_qkv_prompt_kernel, tm=tm, tiles_per_seq=tps),
        grid=(QKV_DIM // tn, m // tm),
        in_specs=[pl.BlockSpec((tm, k), lambda n, i: (i, 0)),
                  pl.BlockSpec((None, k, tn), lambda n, i: (li, 0, n)),
                  pl.BlockSpec((None, QKV_CONV, tn), lambda n, i: (li, 0, n))],
        out_specs=[pl.BlockSpec((tm, tn), lambda n, i: (i, n)),
                   pl.BlockSpec((None, QKV_CONV - 1, tn), lambda n, i: (i // tps, 0, n))],
        out_shape=[jax.ShapeDtypeStruct((m, QKV_DIM), F32),
                   jax.ShapeDtypeStruct((m // seq, QKV_CONV - 1, QKV_DIM), F32)],
        scratch_shapes=[pltpu.VMEM((tm + SUBLANES, tn), F32)],
        compiler_params=_params("arbitrary", "arbitrary"),
        name="qkv_conv_prompt",
    )(x, w, conv_w)


def _qkv_sample_kernel(x_ref, w_ref, cw_ref, s_ref, act_ref, st_ref):
    x = _dot(x_ref[...], w_ref[...])
    cw = cw_ref[...]
    y = cw[3:4] * x
    for j in range(QKV_CONV - 1):
        y = y + cw[j:j + 1] * s_ref[j]
    act_ref[...] = _silu(y)
    st_ref[0] = s_ref[1]
    st_ref[1] = s_ref[2]
    st_ref[2] = x


def _qkv_sample_call(x, w, conv_w, state_t, li, tn):
    m, k = x.shape
    return pl.pallas_call(
        _qkv_sample_kernel,
        grid=(QKV_DIM // tn,),
        in_specs=[pl.BlockSpec((m, k), lambda n: (0, 0)),
                  pl.BlockSpec((None, k, tn), lambda n: (li, 0, n)),
                  pl.BlockSpec((None, QKV_CONV, tn), lambda n: (li, 0, n)),
                  pl.BlockSpec((None, QKV_CONV - 1, m, tn), lambda n: (li, 0, 0, n))],
        out_specs=[pl.BlockSpec((m, tn), lambda n: (0, n)),
                   pl.BlockSpec((QKV_CONV - 1, m, tn), lambda n: (0, 0, n))],
        out_shape=[jax.ShapeDtypeStruct((m, QKV_DIM), F32),
                   jax.ShapeDtypeStruct((QKV_CONV - 1, m, QKV_DIM), F32)],
        compiler_params=_params("parallel"),
        name="qkv_conv_sample",
    )(x, w, conv_w, state_t)


def _sconv_prompt_kernel(x_ref, wb_ref, wc_ref, wh_ref, cw_ref, y_ref, st_ref, us_ref, *,
                         tm, tiles_per_seq):
    i = pl.program_id(1)

    @pl.when(i % tiles_per_seq == 0)
    def _():
        us_ref[0:SUBLANES, :] = jnp.zeros((SUBLANES, us_ref.shape[1]), F32)

    cw = cw_ref[...]
    x = x_ref[...]
    u = _dot(x, wc_ref[...]) * _dot(x, wh_ref[...])
    us_ref[SUBLANES:tm + SUBLANES, :] = u
    conv = cw[2:3] * u
    for j in range(SC_CONV - 1):
        lag = SC_CONV - 1 - j
        conv = conv + cw[j:j + 1] * us_ref[SUBLANES - lag:tm + SUBLANES - lag, :]
    y_ref[...] = (_dot(x, wb_ref[...]) * conv).astype(y_ref.dtype)
    st_ref[...] = us_ref[tm + SUBLANES - (SC_CONV - 1):tm + SUBLANES, :]
    us_ref[0:SUBLANES, :] = us_ref[tm:tm + SUBLANES, :]


def _sconv_prompt_call(x, w_rest, conv_w, li, seq, tm, tn):
    m, k = x.shape
    tps = seq // tm
    cb, cc, ch = REST_SB // tn, REST_SC // tn, REST_SH // tn
    return pl.pallas_call(
        functools.partial(_sconv_prompt_kernel, tm=tm, tiles_per_seq=tps),
        grid=(SC_DIM // tn, m // tm),
        in_specs=[pl.BlockSpec((tm, k), lambda n, i: (i, 0)),
                  pl.BlockSpec((None, k, tn), lambda n, i: (li, 0, cb + n)),
                  pl.BlockSpec((None, k, tn), lambda n, i: (li, 0, cc + n)),
                  pl.BlockSpec((None, k, tn), lambda n, i: (li, 0, ch + n)),
                  pl.BlockSpec((None, SC_CONV, tn), lambda n, i: (li, 0, n))],
        out_specs=[pl.BlockSpec((tm, tn), lambda n, i: (i, n)),
                   pl.BlockSpec((None, SC_CONV - 1, tn), lambda n, i: (i // tps, 0, n))],
        out_shape=[jax.ShapeDtypeStruct((m, SC_DIM), BF16),
                   jax.ShapeDtypeStruct((m // seq, SC_CONV - 1, SC_DIM), F32)],
        scratch_shapes=[pltpu.VMEM((tm + SUBLANES, tn), F32)],
        compiler_params=_params("arbitrary", "arbitrary"),
        name="sconv_prompt",
    )(x, w_rest, w_rest, w_rest, conv_w)


def _sconv_sample_kernel(x_ref, wb_ref, wc_ref, wh_ref, cw_ref, s_ref, y_ref, st_ref):
    x = x_ref[...]
    u = _dot(x, wc_ref[...]) * _dot(x, wh_ref[...])
    cw = cw_ref[...]
    conv = cw[2:3] * u
    for j in range(SC_CONV - 1):
        conv = conv + cw[j:j + 1] * s_ref[j]
    y_ref[...] = (_dot(x, wb_ref[...]) * conv).astype(y_ref.dtype)
    st_ref[0] = s_ref[1]
    st_ref[1] = u


def _sconv_sample_call(x, w_rest, conv_w, state_t, li, tn):
    m, k = x.shape
    cb, cc, ch = REST_SB // tn, REST_SC // tn, REST_SH // tn
    return pl.pallas_call(
        _sconv_sample_kernel,
        grid=(SC_DIM // tn,),
        in_specs=[pl.BlockSpec((m, k), lambda n: (0, 0)),
                  pl.BlockSpec((None, k, tn), lambda n: (li, 0, cb + n)),
                  pl.BlockSpec((None, k, tn), lambda n: (li, 0, cc + n)),
                  pl.BlockSpec((None, k, tn), lambda n: (li, 0, ch + n)),
                  pl.BlockSpec((None, SC_CONV, tn), lambda n: (li, 0, n)),
                  pl.BlockSpec((None, SC_CONV - 1, m, tn), lambda n: (li, 0, 0, n))],
        out_specs=[pl.BlockSpec((m, tn), lambda n: (0, n)),
                   pl.BlockSpec((SC_CONV - 1, m, tn), lambda n: (0, 0, n))],
        out_shape=[jax.ShapeDtypeStruct((m, SC_DIM), BF16),
                   jax.ShapeDtypeStruct((SC_CONV - 1, m, SC_DIM), F32)],
        compiler_params=_params("parallel"),
        name="sconv_sample",
    )(x, w_rest, w_rest, w_rest, conv_w, state_t)


def _gdn_prompt_kernel(q_ref, k_ref, v_ref, z_ref, gp_ref, gn_ref, y_ref, s_ref, *, hb):
    c = CHUNK
    c2 = 2 * c
    npairs = hb // 2

    @pl.when(pl.program_id(2) == 0)
    def _():
        s_ref[...] = jnp.zeros(s_ref.shape, F32)

    ri = lax.broadcasted_iota(jnp.int32, (c2, c2), 0)
    ci = lax.broadcasted_iota(jnp.int32, (c2, c2), 1)
    r_in_rblk = (ri & (c - 1)).astype(jnp.uint32)
    c_in_rblk = (ci - (ri - (ri & (c - 1)))).astype(jnp.uint32)
    r_in_cblk = (ri - (ci - (ci & (c - 1)))).astype(jnp.uint32)
    c_in_cblk = (ci & (c - 1)).astype(jnp.uint32)
    lower = c_in_rblk <= r_in_rblk
    strict = c_in_rblk < r_in_rblk
    same = c_in_rblk < jnp.uint32(c)
    upper = r_in_cblk <= c_in_cblk

    gp = gp_ref[...]
    g = gp[:, :npairs]
    beta = gp[:, npairs:]
    g_col = _dot_hi(lower.astype(F32), g)
    g_row = _dot_tn(g, upper.astype(F32), precision=HI)
    g_last = _dot_hi(same.astype(F32), g)
    e_g = jnp.exp(g_col)
    e_rest = jnp.exp(g_last - g_col)
    e_last = jnp.exp(g_last)
    beta_e_g = beta * e_g
    gn = gn_ref[...]

    pairs = range(npairs)
    heads = [(p, r) for p in pairs for r in range(2)]
    k_st, q_st, k_b, kq = [], [], [], []
    for p in pairs:
        q = q_ref[:, p * HEAD_K:(p + 1) * HEAD_K]
        k = k_ref[:, p * HEAD_K:(p + 1) * HEAD_K]
        qn = q * (lax.rsqrt(jnp.sum(q * q, axis=-1, keepdims=True) + EPS) * (HEAD_K ** -0.5))
        kn = k * lax.rsqrt(jnp.sum(k * k, axis=-1, keepdims=True) + EPS)
        k_st.append(jnp.concatenate([kn, kn], axis=0))
        q_st.append(jnp.concatenate([qn, qn], axis=0))
        k_b.append(k_st[p].astype(BF16))
        kq.append(_dot_nt(jnp.concatenate([k_b[p], q_st[p].astype(BF16)], axis=0), k_b[p]))
    decay = [jnp.exp(jnp.where(lower, g_col[:, p:p + 1] - g_row[p:p + 1, :], -jnp.inf)) for p in pairs]
    a = [jnp.where(strict, beta[:, p:p + 1] * kq[p][:c2] * decay[p], 0.0) for p in pairs]
    qkd_b = [(kq[p][c2:] * decay[p]).astype(BF16) for p in pairs]
    a_b = [x.astype(BF16) for x in a]
    corr = [-x for x in a]
    pw = [_dot(x, x) for x in a_b]
    span = 2
    while span < c:
        pw_b = [x.astype(BF16) for x in pw]
        corr = [corr[p] + pw[p] + _dot(corr[p].astype(BF16), pw_b[p]) for p in pairs]
        span *= 2
        if span < c:
            pw = [_dot(x, x) for x in pw_b]
    sol = []
    for p in pairs:
        v_st = jnp.concatenate([v_ref[:, (2 * p + r) * HEAD_V:(2 * p + r + 1) * HEAD_V]
                                for r in range(2)], axis=0)
        rhs = jnp.concatenate([v_st * beta[:, p:p + 1], k_st[p] * beta_e_g[:, p:p + 1]], axis=1)
        sol.append(rhs + _dot(corr[p].astype(BF16), rhs.astype(BF16)))
    qe_st = [q_st[p] * e_g[:, p:p + 1] for p in pairs]
    kd_b = [(k_st[p] * e_rest[:, p:p + 1]).astype(BF16) for p in pairs]
    s_old = [s_ref[2 * p + r] for p, r in heads]
    ws_qs = [_dot(jnp.concatenate([sol[p][r * c:(r + 1) * c, HEAD_V:], qe_st[p][r * c:(r + 1) * c]],
                                  axis=0).astype(BF16), s_old[2 * p + r].astype(BF16))
             for p, r in heads]
    u_new = [sol[p][r * c:(r + 1) * c, :HEAD_V] - ws_qs[2 * p + r][:c] for p, r in heads]
    u_new_b = [x.astype(BF16) for x in u_new]
    for p, r in heads:
        h = 2 * p + r
        s_ref[h] = (s_old[h] * e_last[r * c:r * c + 1, p:p + 1]
                    + _dot_tn(kd_b[p][r * c:(r + 1) * c], u_new_b[h]))
    for p in pairs:
        o_st = (jnp.concatenate([ws_qs[2 * p][c:], ws_qs[2 * p + 1][c:]], axis=0)
                + _dot(qkd_b[p], jnp.concatenate([u_new_b[2 * p], u_new_b[2 * p + 1]], axis=0)))
        y_st = _rms(o_st, gn)
        for r in range(2):
            cols = slice((2 * p + r) * HEAD_V, (2 * p + r + 1) * HEAD_V)
            y_ref[:, cols] = (y_st[r * c:(r + 1) * c] * z_ref[:, cols]).astype(y_ref.dtype)


def _gate_pair_layout(x, hb):
    rows, heads = x.shape
    x = x.reshape(rows // CHUNK, CHUNK, heads // hb, hb // 2, 2)
    return x.transpose(2, 0, 4, 1, 3).reshape(heads // hb, rows // CHUNK, 2 * CHUNK, hb // 2)


def _gdn_prompt_call(qkv, z_act, gate_pairs, gdn_norm, li, seq, hb):
    m = qkv.shape[0]
    nb, nc = m // seq, seq // CHUNK
    qw, vw = HEAD_K * hb // 2, HEAD_V * hb
    kb, vb = KEY_DIM // qw, 2 * KEY_DIM // vw
    return pl.pallas_call(
        functools.partial(_gdn_prompt_kernel, hb=hb),
        grid=(nb, N_V_HEADS // hb, nc),
        in_specs=[pl.BlockSpec((CHUNK, qw), lambda b, h, c: (b * nc + c, h)),
                  pl.BlockSpec((CHUNK, qw), lambda b, h, c: (b * nc + c, kb + h)),
                  pl.BlockSpec((CHUNK, vw), lambda b, h, c: (b * nc + c, vb + h)),
                  pl.BlockSpec((CHUNK, vw), lambda b, h, c: (b * nc + c, h)),
                  pl.BlockSpec((None, None, 2 * CHUNK, hb), lambda b, h, c: (h, b * nc + c, 0, 0)),
                  pl.BlockSpec((None, 1, HEAD_V), lambda b, h, c: (li, 0, 0))],
        out_specs=[pl.BlockSpec((CHUNK, vw), lambda b, h, c: (b * nc + c, h)),
                   pl.BlockSpec((None, hb, HEAD_K, HEAD_V), lambda b, h, c: (b, h, 0, 0))],
        out_shape=[jax.ShapeDtypeStruct((m, VAL_DIM), BF16),
                   jax.ShapeDtypeStruct((nb, N_V_HEADS, HEAD_K, HEAD_V), F32)],
        compiler_params=_params("parallel", "parallel", "arbitrary"),
        name="gdn_prompt",
    )(qkv, qkv, qkv, z_act, gate_pairs, gdn_norm)


def _lanes_to_sublanes(row):
    n = row.shape[1]
    ri = lax.broadcasted_iota(jnp.int32, (n, n), 0)
    ci = lax.broadcasted_iota(jnp.int32, (n, n), 1)
    diag = jnp.where(ri == ci, jnp.broadcast_to(row, (n, n)), 0.0)
    return jnp.broadcast_to(jnp.sum(diag, axis=1, keepdims=True), (n, n))


def _gdn_sample_kernel(gate_ref, q_ref, k_ref, v_ref, z_ref, gn_ref, s_ref, stacked_ref, y_ref, so_ref,
                       *, hb, bt):
    del stacked_ref
    hblk = pl.program_id(0)
    tile = pl.program_id(1)
    gn = gn_ref[...]
    samples = range(bt)
    q_col, k_col = {}, {}
    for j in range(hb // 2):
        q = q_ref[:, j * HEAD_K:(j + 1) * HEAD_K]
        k = k_ref[:, j * HEAD_K:(j + 1) * HEAD_K]
        qn = q * (lax.rsqrt(jnp.sum(q * q, axis=-1, keepdims=True) + EPS) * (HEAD_K ** -0.5))
        kn = k * lax.rsqrt(jnp.sum(k * k, axis=-1, keepdims=True) + EPS)
        for b in samples:
            q_col[b, j] = _lanes_to_sublanes(qn[b:b + 1])
            k_col[b, j] = _lanes_to_sublanes(kn[b:b + 1])
    for h in range(hb):
        cols = slice(h * HEAD_V, (h + 1) * HEAD_V)
        j = h // 2
        o_rows = []
        for b in samples:
            base = (tile * bt + b) * (2 * N_V_HEADS) + hblk * hb + h
            beta = gate_ref[base]
            e_g = gate_ref[base + N_V_HEADS]
            s = s_ref[b, h]
            s_k = jnp.sum(s * k_col[b, j], axis=0, keepdims=True)
            delta = beta * (v_ref[b:b + 1, cols] - e_g * s_k)
            s_new = e_g * s + k_col[b, j] * delta
            so_ref[b, h] = s_new
            o_rows.append(jnp.sum(s_new * q_col[b, j], axis=0, keepdims=True))
        o = jnp.concatenate(o_rows, axis=0)
        y_ref[:, cols] = _rms(o, gn) * z_ref[:, cols]


def _gdn_sample_call(gates_flat, qkv, z_act, gdn_norm, state, new_states, li, hb, bt):
    m = qkv.shape[0]
    qw, vw = HEAD_K * hb // 2, HEAD_V * hb
    kb, vb = KEY_DIM // qw, 2 * KEY_DIM // vw
    state_spec = pl.BlockSpec((None, bt, hb, HEAD_K, HEAD_V), lambda h, t: (li, t, h, 0, 0))
    in_specs = [pl.BlockSpec(memory_space=pltpu.SMEM),
                pl.BlockSpec((bt, qw), lambda h, t: (t, h)),
                pl.BlockSpec((bt, qw), lambda h, t: (t, kb + h)),
                pl.BlockSpec((bt, vw), lambda h, t: (t, vb + h)),
                pl.BlockSpec((bt, vw), lambda h, t: (t, h)),
                pl.BlockSpec((None, 1, HEAD_V), lambda h, t: (li, 0, 0)),
                state_spec,
                pl.BlockSpec(memory_space=pl.ANY)]
    return pl.pallas_call(
        functools.partial(_gdn_sample_kernel, hb=hb, bt=bt),
        grid=(N_V_HEADS // hb, m // bt),
        in_specs=in_specs,
        out_specs=[pl.BlockSpec((bt, vw), lambda h, t: (t, h)), state_spec],
        out_shape=[jax.ShapeDtypeStruct((m, VAL_DIM), F32),
                   jax.ShapeDtypeStruct(state.shape, F32)],
        input_output_aliases={len(in_specs) - 1: 1},
        compiler_params=_params("parallel", "parallel"),
        name="gdn_sample",
    )(gates_flat, qkv, qkv, qkv, z_act, gdn_norm, state, new_states)


def _merge_kernel(ya_ref, yb_ref, ga_ref, gb_ref, wa_ref, wb_ref, o_ref):
    a = _dot(ya_ref[...].astype(BF16), wa_ref[...])
    b = _dot(yb_ref[...], wb_ref[...])
    o_ref[...] = (ga_ref[...] * a + gb_ref[...] * b).astype(o_ref.dtype)


def _merge_call(ya, yb, gates, w_gdn_out, w_sconv_out, li, tm, tn):
    m = ya.shape[0]
    gbo = D_MODEL // tn
    return pl.pallas_call(
        _merge_kernel,
        grid=(D_MODEL // tn, m // tm),
        in_specs=[pl.BlockSpec((tm, VAL_DIM), lambda n, i: (i, 0)),
                  pl.BlockSpec((tm, SC_DIM), lambda n, i: (i, 0)),
                  pl.BlockSpec((tm, tn), lambda n, i: (i, n)),
                  pl.BlockSpec((tm, tn), lambda n, i: (i, gbo + n)),
                  pl.BlockSpec((None, VAL_DIM, tn), lambda n, i: (li, 0, n)),
                  pl.BlockSpec((None, SC_DIM, tn), lambda n, i: (li, 0, n))],
        out_specs=pl.BlockSpec((tm, tn), lambda n, i: (i, n)),
        out_shape=jax.ShapeDtypeStruct((m, D_MODEL), BF16),
        compiler_params=_params("parallel", "parallel"),
        name="merge",
    )(ya, yb, gates, gates, w_gdn_out, w_sconv_out)


def _oproj_kernel(x_ref, h_ref, w_ref, g_ref, h_out, n_out):
    h = h_ref[...] + _dot(x_ref[...], w_ref[...])
    h_out[...] = h
    n_out[...] = _rms(h, g_ref[...]).astype(n_out.dtype)


def _oproj_call(x, h, w_o, ffn_norm, li, tm):
    m, d = h.shape
    return pl.pallas_call(
        _oproj_kernel,
        grid=(m // tm,),
        in_specs=[pl.BlockSpec((tm, d), lambda i: (i, 0)),
                  pl.BlockSpec((tm, d), lambda i: (i, 0)),
                  pl.BlockSpec((None, d, d), lambda i: (li, 0, 0)),
                  pl.BlockSpec((None, 1, d), lambda i: (li, 0, 0))],
        out_specs=[pl.BlockSpec((tm, d), lambda i: (i, 0)),
                   pl.BlockSpec((tm, d), lambda i: (i, 0))],
        out_shape=[jax.ShapeDtypeStruct((m, d), F32),
                   jax.ShapeDtypeStruct((m, d), BF16)],
        compiler_params=_params("parallel"),
        name="oproj",
    )(x, h, w_o, ffn_norm)


def _ffn_up_kernel(x_ref, wg_ref, wu_ref, o_ref):
    x = x_ref[...]
    o_ref[...] = (_silu(_dot(x, wg_ref[...])) * _dot(x, wu_ref[...])).astype(o_ref.dtype)


def _ffn_up_call(x, w_gate, w_up, li, tm, tn):
    m, k = x.shape
    return pl.pallas_call(
        _ffn_up_kernel,
        grid=(D_FF // tn, m // tm),
        in_specs=[pl.BlockSpec((tm, k), lambda n, i: (i, 0)),
                  pl.BlockSpec((None, k, tn), lambda n, i: (li, 0, n)),
                  pl.BlockSpec((None, k, tn), lambda n, i: (li, 0, n))],
        out_specs=pl.BlockSpec((tm, tn), lambda n, i: (i, n)),
        out_shape=jax.ShapeDtypeStruct((m, D_FF), BF16),
        compiler_params=_params("parallel", "parallel"),
        name="ffn_up",
    )(x, w_gate, w_up)


def _ffn_down_kernel(x_ref, h_ref, w_ref, o_ref):
    o_ref[...] = h_ref[...] + _dot(x_ref[...], w_ref[...])


def _ffn_down_call(x, h, w_down, li, tm, tn):
    m, k = x.shape
    return pl.pallas_call(
        _ffn_down_kernel,
        grid=(D_MODEL // tn, m // tm),
        in_specs=[pl.BlockSpec((tm, k), lambda n, i: (i, 0)),
                  pl.BlockSpec((tm, tn), lambda n, i: (i, n)),
                  pl.BlockSpec((None, k, tn), lambda n, i: (li, 0, n))],
        out_specs=pl.BlockSpec((tm, tn), lambda n, i: (i, n)),
        out_shape=jax.ShapeDtypeStruct((m, D_MODEL), F32),
        compiler_params=_params("parallel", "parallel"),
        name="ffn_down",
    )(x, h, w_down)


def _ple_kernel(h_ref, p_ref, wg_ref, wp_ref, g_ref, gnext_ref, h_out, n_out):
    h = h_ref[...]
    hn = _rms(h, g_ref[...]).astype(BF16)
    gate = jax.nn.sigmoid(_dot(hn, wg_ref[...]))
    h = h + gate * _dot(p_ref[...].astype(BF16), wp_ref[...])
    h_out[...] = h
    n_out[...] = _rms(h, gnext_ref[...]).astype(n_out.dtype)


def _ple_call(h, p, w_gate, w_proj, ple_norm, next_norm, li, next_li, next_dtype, tm):
    m, d = h.shape
    return pl.pallas_call(
        _ple_kernel,
        grid=(m // tm,),
        in_specs=[pl.BlockSpec((tm, d), lambda i: (i, 0)),
                  pl.BlockSpec((None, tm, PLE_DIM), lambda i: (li, i, 0)),
                  pl.BlockSpec((None, d, d), lambda i: (li, 0, 0)),
                  pl.BlockSpec((None, PLE_DIM, d), lambda i: (li, 0, 0)),
                  pl.BlockSpec((None, 1, d), lambda i: (li, 0, 0)),
                  pl.BlockSpec((None, 1, d), lambda i: (next_li, 0, 0))],
        out_specs=[pl.BlockSpec((tm, d), lambda i: (i, 0)),
                   pl.BlockSpec((tm, d), lambda i: (i, 0))],
        out_shape=[jax.ShapeDtypeStruct((m, d), F32),
                   jax.ShapeDtypeStruct((m, d), next_dtype)],
        compiler_params=_params("parallel"),
        name="ple",
    )(h, p, w_gate, w_proj, ple_norm, next_norm)


def _layer(li, h, xn, p, w, seq, states):
    m = h.shape[0]
    prompt = seq > 1
    tm = 512 if prompt else m
    tm_row = 256 if prompt else m
    tm_big = 1024 if prompt else m
    hb = GDN_PROMPT_HEADS_PER_STEP if prompt else GDN_SAMPLE_HEADS_PER_STEP

    gates3 = _gates_call(xn, w["ab"], w["a_log"], w["dt_bias"], li, tm)
    z_act = _proj_act_call(xn, w["main"], li, OFF_Z, VAL_DIM, _silu, tm_big, 2048, "proj_z")
    mix_gates = _proj_act_call(xn, w["rest"], li, REST_G, 2 * D_MODEL, jax.nn.sigmoid, tm_big, 2048,
                               "proj_mix_gates")

    if prompt:
        qkv, new_qkv = _qkv_prompt_call(xn, w["main"], w["qkv_conv"], li, seq, tm, 2048)
        yb, new_sc = _sconv_prompt_call(xn, w["rest"], w["sconv"], li, seq, tm, 1024)
        gate_pairs = jnp.concatenate(
            [_gate_pair_layout(gates3[:, :N_V_HEADS], hb),
             _gate_pair_layout(gates3[:, N_V_HEADS:2 * N_V_HEADS], hb)], axis=-1)
        ya, new_gdn = _gdn_prompt_call(qkv, z_act, gate_pairs, w["gdn_norm"], li, seq, hb)
    else:
        s_gdn, s_qkv_t, s_sc_t, new_gdn_stacked = states
        qkv, new_qkv_t = _qkv_sample_call(xn, w["main"], w["qkv_conv"], s_qkv_t, li, 1024)
        yb, new_sc_t = _sconv_sample_call(xn, w["rest"], w["sconv"], s_sc_t, li, 512)
        new_qkv = new_qkv_t.transpose(1, 0, 2)
        new_sc = new_sc_t.transpose(1, 0, 2)
        gates_flat = gates3[:, N_V_HEADS:].reshape(-1)
        ya, new_gdn = _gdn_sample_call(gates_flat, qkv, z_act, w["gdn_norm"], s_gdn, new_gdn_stacked,
                                       li, hb, SAMPLE_TILE)

    mixed = _merge_call(ya, yb, mix_gates, w["gdn_out"], w["sconv_out"], li, tm_row, 1024)
    h2, hn2 = _oproj_call(mixed, h, w["o"], w["ffn_norm"], li, tm)
    act = _ffn_up_call(hn2, w["ffn_gate"], w["ffn_up"], li, tm_big, D_FF // 4)
    h3 = _ffn_down_call(act, h2, w["ffn_down"], li, tm, 1024)
    last = li == DEPTH - 1
    next_norm = w["final_norm"] if last else w["attn_norm"]
    h4, nxt = _ple_call(h3, p, w["ple_gate"], w["ple_proj"], w["ple_norm"], next_norm, li,
                        0 if last else li + 1, F32 if last else BF16, tm)
    return h4, nxt, new_gdn, new_qkv, new_sc


def kernel(x_prompt, x_sample, state_gdn, state_qkv_conv, state_short_conv, p_prompt, p_sample, attn_norm, w_in, qkv_conv_w, a_log, dt_bias, gdn_norm, w_gdn_out, sconv_w, w_sconv_out, w_o, ffn_norm, w_ffn_gate, w_ffn_up, w_ffn_down, ple_norm, w_ple_gate, w_ple_proj, final_norm):
    nb, seq, d = x_prompt.shape
    ns = x_sample.shape[0]
    w_in_t = jnp.swapaxes(w_in, 1, 2)
    w = {
        "main": _cast_columns_call(w_in_t, 0, OFF_AB, 2048, 1024),
        "ab": _cast_columns_call(w_in_t, OFF_AB, LANES, 2048, LANES),
        "rest": _cast_columns_call(w_in_t, OFF_REST, w_in.shape[2] - OFF_REST, 2048, 512),
        "qkv_conv": qkv_conv_w,
        "sconv": sconv_w,
        "a_log": a_log[:, None, :],
        "dt_bias": dt_bias[:, None, :],
        "gdn_norm": gdn_norm[:, None, :],
        "gdn_out": w_gdn_out.astype(BF16),
        "sconv_out": w_sconv_out.astype(BF16),
        "o": w_o.astype(BF16),
        "ffn_gate": w_ffn_gate.astype(BF16),
        "ffn_up": w_ffn_up.astype(BF16),
        "ffn_down": w_ffn_down.astype(BF16),
        "ple_gate": w_ple_gate.astype(BF16),
        "ple_proj": w_ple_proj.astype(BF16),
        "attn_norm": attn_norm[:, None, :],
        "ffn_norm": ffn_norm[:, None, :],
        "ple_norm": ple_norm[:, None, :],
        "final_norm": final_norm[None, None, :],
    }
    hp = x_prompt.reshape(nb * seq, d)
    hs = x_sample.reshape(ns, d)
    pp = p_prompt.reshape(DEPTH, nb * seq, PLE_DIM)
    ps = p_sample.reshape(DEPTH, ns, PLE_DIM)
    s_qkv_t = state_qkv_conv.transpose(0, 2, 1, 3)
    s_sc_t = state_short_conv.transpose(0, 2, 1, 3)

    xp = _rmsnorm_call(hp, w["attn_norm"], 0, 256)
    xs = _rmsnorm_call(hs, w["attn_norm"], 0, ns)
    outs_p, outs_s = [], []
    new_gdn_s = jnp.zeros(state_gdn.shape, F32)
    for li in range(DEPTH):
        hp, xp, *st = _layer(li, hp, xp, pp, w, seq, None)
        outs_p.append(st)
        hs, xs, new_gdn_s, *st = _layer(li, hs, xs, ps, w, 1, (state_gdn, s_qkv_t, s_sc_t, new_gdn_s))
        outs_s.append(st)
    y_prompt = xp.reshape(nb, seq, d)
    y_sample = xs.reshape(ns, 1, d)
    stack = lambda outs, j: jnp.stack([o[j] for o in outs])
    return (y_prompt, y_sample, stack(outs_p, 0), stack(outs_p, 1), stack(outs_p, 2),
            new_gdn_s, stack(outs_s, 0), stack(outs_s, 1))
```
